```python
import jax, jax.numpy as jnp
from jax import lax
import numpy as np

D_MODEL = 2048
BATCH = 1
SEQ = 8192
DEPTH = 4

GRID_W = 64
CTX_LEN = 256
N_MIXERS = 3
EPS = 1e-6
NA_HEADS = 16
NA_HEAD_DIM = D_MODEL // NA_HEADS
NA_KH_MAX = 8
NA_KW = 16
GQA_HEADS = 16
GQA_KV_HEADS = 4
GQA_HEAD_DIM = 128
ROPE_THETA = 10000.0
Q_BLOCK = 128
D_FF = 5632
N_EXPERTS = 8
TOP_K = 2
N_LAYERS_A = (DEPTH + 2) // 3
N_LAYERS_B = (DEPTH + 1) // 3
N_LAYERS_C = DEPTH // 3
N_DENSE = (DEPTH + 1) // 2
N_MOE = DEPTH // 2

kernel_name = "hybrid_interleaved_dit_block"


def rms_norm(x, g):
    x32 = x.astype(jnp.float32)
    y = x32 * lax.rsqrt(jnp.mean(x32 * x32, axis=-1, keepdims=True) + EPS)
    return (y * g.astype(jnp.float32)).astype(x.dtype)


def modulate(h, shift, scale):
    return h * (1 + scale[:, None, :]) + shift[:, None, :]


def short_conv_mixer(h, w_in, conv_w, w_out):
    b_gate, c_gate, v = jnp.split(h @ w_in, 3, axis=-1)
    u = jnp.pad(c_gate * v, ((0, 0), (1, 1), (0, 0)))
    conv = conv_w[0] * u[:, :-2] + conv_w[1] * u[:, 1:-1] + conv_w[2] * u[:, 2:]
    return (b_gate * conv) @ w_out


def na_mixer(h, hc, w_qkv, rpb, w_out):
    B, L, D = h.shape
    rows = L // GRID_W
    kh = min(NA_KH_MAX, rows)
    H, hd = NA_HEADS, NA_HEAD_DIM
    scale = hd ** -0.5
    q, k, v = jnp.split(h @ w_qkv, 3, axis=-1)
    q = q.reshape(B, rows, GRID_W, H, hd)
    k = k.reshape(B, rows, GRID_W, H, hd)
    v = v.reshape(B, rows, GRID_W, H, hd)
    qc, kc, vc = jnp.split(hc @ w_qkv, 3, axis=-1)
    Lc = hc.shape[1]
    qc, kc, vc = (t.reshape(B, Lc, H, hd) for t in (qc, kc, vc))
    s_cc = jnp.einsum('bqhd,bkhd->bhqk', qc, kc).astype(jnp.float32) * scale
    p_cc = jax.nn.softmax(s_cc, axis=-1).astype(vc.dtype)
    o_c = jnp.einsum('bhqk,bkhd->bqhd', p_cc, vc).reshape(B, Lc, D)
    cols = jnp.arange(GRID_W)
    col0 = jnp.clip(cols - NA_KW // 2, 0, GRID_W - NA_KW)
    col_idx = col0[:, None] + jnp.arange(NA_KW)[None, :]
    dc = col_idx - cols[:, None] + (NA_KW - 1)

    def row_block(r):
        r0 = jnp.clip(r - kh // 2, 0, rows - kh)
        q_r = lax.dynamic_index_in_dim(q, r, axis=1, keepdims=False)
        k_band = lax.dynamic_slice_in_dim(k, r0, kh, axis=1)
        v_band = lax.dynamic_slice_in_dim(v, r0, kh, axis=1)
        k_win = k_band[:, :, col_idx]
        v_win = v_band[:, :, col_idx]
        dr = r0 + jnp.arange(kh) - r + (NA_KH_MAX - 1)
        bias = rpb[:, dr[:, None, None], dc[None, :, :]]
        s_nb = jnp.einsum('bqhd,biqjhd->bhqij', q_r, k_win).astype(jnp.float32) * scale
        s_nb = s_nb + bias.transpose(0, 2, 1, 3)[None].astype(jnp.float32)
        s_nb = s_nb.reshape(B, H, GRID_W, kh * NA_KW)
        s_cx = jnp.einsum('bqhd,bkhd->bhqk', q_r, kc).astype(jnp.float32) * scale
        p = jax.nn.softmax(jnp.concatenate([s_nb, s_cx], axis=-1), axis=-1).astype(v.dtype)
        p_nb = p[..., :kh * NA_KW].reshape(B, H, GRID_W, kh, NA_KW)
        p_cx = p[..., kh * NA_KW:]
        return (jnp.einsum('bhqij,biqjhd->bqhd', p_nb, v_win)
                + jnp.einsum('bhqk,bkhd->bqhd', p_cx, vc))

    o = lax.map(row_block, jnp.arange(rows))
    o = o.transpose(1, 0, 2, 3, 4).reshape(B, L, D)
    return o @ w_out, o_c @ w_out


def axial_rope_tables(L, hd):
    t = jnp.arange(L)
    row = (t // GRID_W).astype(jnp.float32)
    col = (t % GRID_W).astype(jnp.float32)
    quarter = hd // 4
    inv_freq = ROPE_THETA ** (-jnp.arange(quarter, dtype=jnp.float32) / quarter)
    ang = jnp.stack([row[:, None] * inv_freq, col[:, None] * inv_freq], axis=1)
    return jnp.cos(ang), jnp.sin(ang)


def apply_axial_rope(x, cos, sin):
    B, L, H, hd = x.shape
    xa = x.reshape(B, L, H, 2, 2, hd // 4)
    x1, x2 = xa[..., 0, :], xa[..., 1, :]
    c = cos[None, :, None].astype(x.dtype)
    s = sin[None, :, None].astype(x.dtype)
    out = jnp.stack([x1 * c - x2 * s, x1 * s + x2 * c], axis=-2)
    return out.reshape(B, L, H, hd)


def gqa_mixer(h, hc, w_q, w_kv, q_norm, k_norm, w_out, cos, sin):
    B, L, D = h.shape
    Lc = hc.shape[1]
    Hq, Hkv, hd = GQA_HEADS, GQA_KV_HEADS, GQA_HEAD_DIM
    G = Hq // Hkv
    scale = hd ** -0.5
    q = rms_norm((h @ w_q).reshape(B, L, Hq, hd), q_norm)
    k, v = jnp.split(h @ w_kv, 2, axis=-1)
    k = rms_norm(k.reshape(B, L, Hkv, hd), k_norm)
    v = v.reshape(B, L, Hkv, hd)
    q = apply_axial_rope(q, cos, sin)
    k = apply_axial_rope(k, cos, sin)
    qc = rms_norm((hc @ w_q).reshape(B, Lc, Hq, hd), q_norm).reshape(B, Lc, Hkv, G, hd)
    kc, vc = jnp.split(hc @ w_kv, 2, axis=-1)
    kc = rms_norm(kc.reshape(B, Lc, Hkv, hd), k_norm)
    vc = vc.reshape(B, Lc, Hkv, hd)
    s_cc = jnp.einsum('bqkgd,bskd->bkgqs', qc, kc).astype(jnp.float32) * scale
    p_cc = jax.nn.softmax(s_cc, axis=-1).astype(vc.dtype)
    o_c = jnp.einsum('bkgqs,bskd->bqkgd', p_cc, vc).reshape(B, Lc, Hq * hd)
    k_all = jnp.concatenate([k, kc], axis=1)
    v_all = jnp.concatenate([v, vc], axis=1)
    n_blk = L // Q_BLOCK
    qb = q.reshape(B, n_blk, Q_BLOCK, Hkv, G, hd).transpose(1, 0, 2, 3, 4, 5)

    def block(qi):
        s = jnp.einsum('bqkgd,bskd->bkgqs', qi, k_all).astype(jnp.float32) * scale
        p = jax.nn.softmax(s, axis=-1).astype(v_all.dtype)
        return jnp.einsum('bkgqs,bskd->bqkgd', p, v_all)

    o = lax.map(block, qb).transpose(1, 0, 2, 3, 4, 5).reshape(B, L, Hq * hd)
    return o @ w_out, o_c @ w_out


def swiglu(h, w_gu, w_down):
    g, u = jnp.split(h @ w_gu, 2, axis=-1)
    return (jax.nn.silu(g) * u) @ w_down


def moe_swiglu(h, w_router, w_gu, w_down):
    logits = (h @ w_router).astype(jnp.float32)
    top_v, top_i = lax.top_k(logits, TOP_K)
    top_w = jax.nn.softmax(top_v, axis=-1)
    gates = jnp.sum(jax.nn.one_hot(top_i, N_EXPERTS, dtype=jnp.float32) * top_w[..., None], axis=-2)
    gates = gates.astype(h.dtype)
    out = jnp.zeros_like(h)
    for e in range(N_EXPERTS):
        out = out + gates[..., e:e + 1] * swiglu(h, w_gu[e], w_down[e])
    return out


def setup_inputs(seed: int = 0) -> dict:
    key = jax.random.key(seed)
    ks = jax.random.split(key, 32)
    D, F = D_MODEL, D_FF
    f32 = jnp.float32

    def nrm(k, shape, scale):
        return jax.random.normal(k, shape, f32) * scale

    def gain(k, shape):
        return 1.0 + 0.02 * jax.random.normal(k, shape, f32)

    return {
        "x": nrm(ks[0], (BATCH, SEQ, D), 1.0),
        "c": nrm(ks[1], (BATCH, D), 1.0),
        "ctx": nrm(ks[2], (BATCH, CTX_LEN, D), 1.0),
        "c_ctx": nrm(ks[3], (D,), 1.0),
        "ada_w": nrm(ks[4], (DEPTH, D, 6 * D), 0.5 * D ** -0.5),
        "ada_b": nrm(ks[5], (DEPTH, 6 * D), 0.02),
        "norm_mix": gain(ks[6], (DEPTH, D)),
        "norm_ffn": gain(ks[7], (DEPTH, D)),
        "norm_final": gain(ks[8], (D,)),
        "conv_w_in": nrm(ks[9], (N_LAYERS_A, D, 3 * D), D ** -0.5),
        "conv_w": nrm(ks[10], (N_LAYERS_A, 3, D), 3 ** -0.5),
        "conv_w_out": nrm(ks[11], (N_LAYERS_A, D, D), D ** -0.5),
        "na_w_qkv": nrm(ks[12], (N_LAYERS_B, D, 3 * D), D ** -0.5),
        "na_rpb": nrm(ks[13], (N_LAYERS_B, NA_HEADS, 2 * NA_KH_MAX - 1, 2 * NA_KW - 1), 0.1),
        "na_w_out": nrm(ks[14], (N_LAYERS_B, D, D), D ** -0.5),
        "gqa_w_q": nrm(ks[15], (N_LAYERS_C, D, GQA_HEADS * GQA_HEAD_DIM), D ** -0.5),
        "gqa_w_kv": nrm(ks[16], (N_LAYERS_C, D, 2 * GQA_KV_HEADS * GQA_HEAD_DIM), D ** -0.5),
        "gqa_q_norm": gain(ks[17], (N_LAYERS_C, GQA_HEAD_DIM)),
        "gqa_k_norm": gain(ks[18], (N_LAYERS_C, GQA_HEAD_DIM)),
        "gqa_w_out": nrm(ks[19], (N_LAYERS_C, GQA_HEADS * GQA_HEAD_DIM, D), (GQA_HEADS * GQA_HEAD_DIM) ** -0.5),
        "ffn_w_gu": nrm(ks[20], (N_DENSE, D, 2 * F), D ** -0.5),
        "ffn_w_down": nrm(ks[21], (N_DENSE, F, D), F ** -0.5),
        "moe_w_router": nrm(ks[22], (N_MOE, D, N_EXPERTS), D ** -0.5),
        "moe_w_gu": nrm(ks[23], (N_MOE, N_EXPERTS, D, 2 * F), D ** -0.5),
        "moe_w_down": nrm(ks[24], (N_MOE, N_EXPERTS, F, D), F ** -0.5),
    }


def reference(x, c, ctx, c_ctx, ada_w, ada_b, norm_mix, norm_ffn, norm_final,
              conv_w_in, conv_w, conv_w_out,
              na_w_qkv, na_rpb, na_w_out,
              gqa_w_q, gqa_w_kv, gqa_q_norm, gqa_k_norm, gqa_w_out,
              ffn_w_gu, ffn_w_down,
              moe_w_router, moe_w_gu, moe_w_down):
    B, L, D = x.shape
    Lc = ctx.shape[1]
    cos, sin = axial_rope_tables(L, GQA_HEAD_DIM)
    s_lat = jax.nn.silu(c)
    s_ctx = jax.nn.silu(c_ctx)[None]
    h_lat, h_ctx = x, ctx
    for i in range(DEPTH):
        need_ctx = i < DEPTH - 1
        sh1, sc1, g1, sh2, sc2, g2 = jnp.split(s_lat @ ada_w[i] + ada_b[i], 6, axis=-1)
        csh1, csc1, cg1, csh2, csc2, cg2 = jnp.split(s_ctx @ ada_w[i] + ada_b[i], 6, axis=-1)
        a = modulate(rms_norm(h_lat, norm_mix[i]), sh1, sc1)
        ac = modulate(rms_norm(h_ctx, norm_mix[i]), csh1, csc1)
        kind = i % N_MIXERS
        j = i // N_MIXERS
        if kind == 0:
            y = short_conv_mixer(a, conv_w_in[j], conv_w[j], conv_w_out[j])
            yc = short_conv_mixer(ac, conv_w_in[j], conv_w[j], conv_w_out[j]) if need_ctx else None
        elif kind == 1:
            y, yc = na_mixer(a, ac, na_w_qkv[j], na_rpb[j], na_w_out[j])
        else:
            y, yc = gqa_mixer(a, ac, gqa_w_q[j], gqa_w_kv[j], gqa_q_norm[j], gqa_k_norm[j],
                              gqa_w_out[j], cos, sin)
        h_lat = h_lat + g1[:, None, :] * y
        if need_ctx:
            h_ctx = h_ctx + cg1[:, None, :] * yc
        f = modulate(rms_norm(h_lat, norm_ffn[i]), sh2, sc2)
        if need_ctx:
            fc = modulate(rms_norm(h_ctx, norm_ffn[i]), csh2, csc2)
            fc = jnp.broadcast_to(fc, (B, Lc, D))
            f_all = jnp.concatenate([fc, f], axis=1)
        else:
            f_all = f
        if i % 2 == 0:
            out_all = swiglu(f_all, ffn_w_gu[i // 2], ffn_w_down[i // 2])
        else:
            out_all = moe_swiglu(f_all, moe_w_router[i // 2], moe_w_gu[i // 2], moe_w_down[i // 2])
        if need_ctx:
            h_ctx = h_ctx + cg2[:, None, :] * out_all[:, :Lc]
            h_lat = h_lat + g2[:, None, :] * out_all[:, Lc:]
        else:
            h_lat = h_lat + g2[:, None, :] * out_all
    return rms_norm(h_lat, norm_final)
```

```python
import functools

import numpy as np
import jax
import jax.numpy as jnp
from jax import lax
from jax.experimental import pallas as pl
from jax.experimental.pallas import tpu as pltpu

F32 = jnp.float32
BF16 = jnp.bfloat16
I32 = jnp.int32

EPS = 1e-6
GRID_W = 64
N_MIXERS = 3
NA_HEADS = 16
NA_KH = 8
NA_KW = 16
NA_QROWS = 4
NA_WROWS = NA_QROWS + NA_KH
GQA_HEADS = 16
GQA_KV_HEADS = 4
HEAD_DIM = 128
ROPE_THETA = 10000.0
N_EXPERTS = 8
TOP_K = 2
LANES = 128
ROW_BLOCK = 256
MASKED = -1e30
VMEM_LIMIT = 60 * 1024 * 1024


def _cparams(*sem):
    return pltpu.CompilerParams(dimension_semantics=sem, vmem_limit_bytes=VMEM_LIMIT)


def _pick(n, cands):
    for c in cands:
        if n % c == 0:
            return c
    raise ValueError(f"no tile for {n} in {cands}")


def _dot(a, b):
    return jnp.dot(a, b, preferred_element_type=F32)


def _dot_nt(a, b):
    return lax.dot_general(a, b, (((1,), (1,)), ((), ())), preferred_element_type=F32)


def _silu(x):
    return x * (1.0 / (1.0 + jnp.exp(-x)))


def _rms(x, gain):
    return x * lax.rsqrt(jnp.mean(x * x, axis=-1, keepdims=True) + EPS) * gain


def _ada_kernel(c_ref, w_ref, b_ref, o_ref):
    s = _silu(c_ref[...])
    o_ref[...] = jnp.dot(s, w_ref[...], precision=lax.Precision.HIGHEST,
                         preferred_element_type=F32) + b_ref[...]


def _ada(cvec, ada_w, ada_b):
    depth, d, n = ada_w.shape
    tn = _pick(n, (1024, 512, 256, 128))
    return pl.pallas_call(
        _ada_kernel,
        grid=(depth, n // tn),
        in_specs=[pl.BlockSpec((8, d), lambda i, j: (0, 0)),
                  pl.BlockSpec((None, d, tn), lambda i, j: (i, 0, j)),
                  pl.BlockSpec((None, 1, tn), lambda i, j: (i, 0, j))],
        out_specs=pl.BlockSpec((None, 8, tn), lambda i, j: (i, 0, j)),
        out_shape=jax.ShapeDtypeStruct((depth, 8, n), F32),
        compiler_params=_cparams("arbitrary", "arbitrary"),
        name="ada",
    )(cvec, ada_w, ada_b.reshape(depth, 1, n))


def _norm_mod_kernel(h_ref, g_ref, ss_ref, a_ref):
    y = _rms(h_ref[...], g_ref[...])
    a_ref[...] = (y * (1.0 + ss_ref[1:2, :]) + ss_ref[0:1, :]).astype(a_ref.dtype)


def _seg_map(n_ctx_blocks):
    return lambda i: (jnp.where(i < n_ctx_blocks, 0, 1), 0, 0)


def _norm_mod(h, gain, ss, n_ctx):
    m, d = h.shape
    tm = ROW_BLOCK
    return pl.pallas_call(
        _norm_mod_kernel,
        grid=(m // tm,),
        in_specs=[pl.BlockSpec((tm, d), lambda i: (i, 0)),
                  pl.BlockSpec((1, d), lambda i: (0, 0)),
                  pl.BlockSpec((None, 2, d), _seg_map(n_ctx // tm))],
        out_specs=pl.BlockSpec((tm, d), lambda i: (i, 0)),
        out_shape=jax.ShapeDtypeStruct((m, d), BF16),
        compiler_params=_cparams("arbitrary"),
        name="norm_mod",
    )(h, gain.reshape(1, d), ss)


def _router_kernel(h_ref, g_ref, ss_ref, wr_ref, ids_ref, wts_ref):
    y = _rms(h_ref[...], g_ref[...])
    f = y * (1.0 + ss_ref[1:2, :]) + ss_ref[0:1, :]
    logits = jnp.dot(f, wr_ref[...], precision=lax.Precision.HIGHEST, preferred_element_type=F32)
    lane = lax.broadcasted_iota(I32, logits.shape, 1)
    neg = jnp.float32(-jnp.inf)
    l1 = jnp.where(lane < N_EXPERTS, logits, neg)
    m1 = jnp.max(l1, axis=-1, keepdims=True)
    i1 = jnp.min(jnp.where(l1 == m1, lane, LANES), axis=-1, keepdims=True)
    l2 = jnp.where(lane == i1, neg, l1)
    m2 = jnp.max(l2, axis=-1, keepdims=True)
    i2 = jnp.min(jnp.where(l2 == m2, lane, LANES), axis=-1, keepdims=True)
    e2 = jnp.exp(m2 - m1)
    w1 = 1.0 / (1.0 + e2)
    w2 = e2 / (1.0 + e2)
    ids_ref[...] = jnp.where(lane == 0, i1, jnp.where(lane == 1, i2, 0))
    wts_ref[...] = jnp.where(lane == 0, w1, jnp.where(lane == 1, w2, 0.0))


def _router(h, gain, ss, w_router, n_ctx):
    m, d = h.shape
    tm = ROW_BLOCK
    wr = jnp.zeros((d, LANES), F32).at[:, :N_EXPERTS].set(w_router)
    return pl.pallas_call(
        _router_kernel,
        grid=(m // tm,),
        in_specs=[pl.BlockSpec((tm, d), lambda i: (i, 0)),
                  pl.BlockSpec((1, d), lambda i: (0, 0)),
                  pl.BlockSpec((None, 2, d), _seg_map(n_ctx // tm)),
                  pl.BlockSpec((d, LANES), lambda i: (0, 0))],
        out_specs=[pl.BlockSpec((tm, LANES), lambda i: (i, 0)),
                   pl.BlockSpec((tm, LANES), lambda i: (i, 0))],
        out_shape=[jax.ShapeDtypeStruct((m, LANES), I32),
                   jax.ShapeDtypeStruct((m, LANES), F32)],
        compiler_params=_cparams("arbitrary"),
        name="router",
    )(h, gain.reshape(1, d), ss, wr)


def _final_norm_kernel(h_ref, g_ref, o_ref):
    o_ref[...] = _rms(h_ref[...], g_ref[...])


def _final_norm(h, gain):
    m, d = h.shape
    tm = ROW_BLOCK
    return pl.pallas_call(
        _final_norm_kernel,
        grid=(m // tm,),
        in_specs=[pl.BlockSpec((tm, d), lambda i: (i, 0)),
                  pl.BlockSpec((1, d), lambda i: (0, 0))],
        out_specs=pl.BlockSpec((tm, d), lambda i: (i, 0)),
        out_shape=jax.ShapeDtypeStruct((m, d), F32),
        compiler_params=_cparams("arbitrary"),
        name="final_norm",
    )(h, gain.reshape(1, d))


def _mm(x, w, *, sel, col_offs, n_tiles, tn, tm, epi, extra=(), extra_specs=(), out_shape, out_specs,
        group=None, w_single_buffer=False, name):
    m_rows, k = x.shape
    n_w = len(col_offs)
    n_pf = 0 if group is None else 2
    n_ex = len(extra)
    n_out = len(out_shape)

    def kern(*refs):
        pf = refs[:n_pf]
        x_ref = refs[n_pf]
        w_refs = refs[n_pf + 1:n_pf + 1 + n_w]
        ex = refs[n_pf + 1 + n_w:n_pf + 1 + n_w + n_ex]
        outs = refs[n_pf + 1 + n_w + n_ex:n_pf + 1 + n_w + n_ex + n_out]
        wb = refs[-1]
        m = pl.program_id(1)
        if group is None:
            is_new = m == 0
        else:
            te, nu = pf
            is_new = jnp.logical_or(m == 0, te[m] != te[jnp.maximum(m - 1, 0)])

        @pl.when(is_new)
        def _():
            for j in range(n_w):
                wb[j] = w_refs[j][...].astype(BF16)

        def compute():
            xv = x_ref[...]
            accs = [_dot(xv, wb[j]) for j in range(n_w)]
            for o, r in zip(outs, epi(accs, m, *ex)):
                o[...] = r.astype(o.dtype)

        if group is None:
            compute()
        else:
            active = m < nu[0]
            pl.when(active)(compute)

            @pl.when(jnp.logical_not(active))
            def _():
                for o in outs:
                    o[...] = jnp.zeros(o.shape, o.dtype)

    def w_map(off):
        if group is None:
            return lambda n, m: (sel, 0, n + off)
        return lambda n, m, te, nu: (te[m], 0, n + off)

    w_kw = {"pipeline_mode": pl.Buffered(1)} if w_single_buffer else {}
    in_specs = ([pl.BlockSpec((tm, k), lambda n, m, *pf: (m, 0))]
                + [pl.BlockSpec((None, k, tn), w_map(off), **w_kw) for off in col_offs]
                + list(extra_specs))
    grid = (n_tiles, m_rows // tm)
    scratch = [pltpu.VMEM((n_w, k, tn), BF16)]
    cp = _cparams("arbitrary", "arbitrary")
    if group is None:
        return pl.pallas_call(kern, grid=grid, in_specs=in_specs, out_specs=out_specs, out_shape=out_shape,
                              scratch_shapes=scratch, compiler_params=cp, name=name)(x, *([w] * n_w), *extra)
    gs = pltpu.PrefetchScalarGridSpec(num_scalar_prefetch=2, grid=grid, in_specs=in_specs,
                                      out_specs=out_specs, scratch_shapes=scratch)
    return pl.pallas_call(kern, grid_spec=gs, out_shape=out_shape, compiler_params=cp,
                          name=name)(group[0], group[1], x, *([w] * n_w), *extra)


def _row_tile(m_rows):
    return _pick(m_rows, (768, 512, 256))


def _proj_plain(x, w, sel, col0, n_cols, name):
    m_rows = x.shape[0]
    tn = _pick(n_cols, (1024, 512, 256))
    assert col0 % tn == 0
    tm = _row_tile(m_rows)
    return _mm(x, w, sel=sel, col_offs=(col0 // tn,), n_tiles=n_cols // tn, tn=tn, tm=tm,
               epi=lambda accs, m: (accs[0],),
               out_shape=[jax.ShapeDtypeStruct((m_rows, n_cols), BF16)],
               out_specs=[pl.BlockSpec((tm, tn), lambda n, m: (m, n))], name=name)[0]


def _proj_conv_in(x, w, sel):
    m_rows = x.shape[0]
    d3 = w.shape[2] // 3
    tn = _pick(d3, (512, 256))
    tm = _row_tile(m_rows)
    nt = d3 // tn
    o = jax.ShapeDtypeStruct((m_rows, d3), BF16)
    spec = pl.BlockSpec((tm, tn), lambda n, m: (m, n))
    return _mm(x, w, sel=sel, col_offs=(0, nt, 2 * nt), n_tiles=nt, tn=tn, tm=tm,
               epi=lambda accs, m: (accs[0], accs[1] * accs[2]),
               out_shape=[o, o], out_specs=[spec, spec], name="conv_in")


def _proj_gu(x, w, sel, group=None):
    m_rows = x.shape[0]
    f = w.shape[2] // 2
    tn = _pick(f, (512, 256))
    tm = 512 if group is not None else _row_tile(m_rows)
    nt = f // tn
    return _mm(x, w, sel=sel, col_offs=(0, nt), n_tiles=nt, tn=tn, tm=tm,
               epi=lambda accs, m: (_silu(accs[0]) * accs[1],),
               out_shape=[jax.ShapeDtypeStruct((m_rows, f), BF16)],
               out_specs=[pl.BlockSpec((tm, tn), lambda n, m, *pf: (m, n))],
               group=group, name="ffn_gu" if group is None else "moe_gu")[0]


def _proj_qk_rope(x, w, sel, col0, n_cols, gain, cos_t, sin_t, scale, name):
    m_rows = x.shape[0]
    tn = _pick(n_cols, (512, 256, 128))
    assert col0 % tn == 0
    tm = _row_tile(m_rows)

    def epi(accs, m, gain_ref, cos_ref, sin_ref):
        lane = lax.broadcasted_iota(I32, (1, HEAD_DIM), 1)
        first = (lane & (HEAD_DIM // 4)) == 0
        cos_v, sin_v, g = cos_ref[...], sin_ref[...], gain_ref[...]
        heads = []
        for hh in range(tn // HEAD_DIM):
            y = _rms(accs[0][:, hh * HEAD_DIM:(hh + 1) * HEAD_DIM], g)
            rot = jnp.where(first, pltpu.roll(y, HEAD_DIM - HEAD_DIM // 4, 1), pltpu.roll(y, HEAD_DIM // 4, 1))
            heads.append((y * cos_v + rot * sin_v) * scale)
        return (jnp.concatenate(heads, axis=1),)

    return _mm(x, w, sel=sel, col_offs=(col0 // tn,), n_tiles=n_cols // tn, tn=tn, tm=tm, epi=epi,
               extra=(gain.reshape(1, HEAD_DIM), cos_t, sin_t),
               extra_specs=(pl.BlockSpec((1, HEAD_DIM), lambda n, m: (0, 0)),
                            pl.BlockSpec((tm, HEAD_DIM), lambda n, m: (m, 0)),
                            pl.BlockSpec((tm, HEAD_DIM), lambda n, m: (m, 0))),
               out_shape=[jax.ShapeDtypeStruct((m_rows, n_cols), BF16)],
               out_specs=[pl.BlockSpec((tm, tn), lambda n, m: (m, n))], name=name)[0]


def _proj_resid(x, w, sel, h, gate, n_ctx, name):
    m_rows, k = x.shape
    n_cols = w.shape[2]
    big_k = k > 4096
    tn = 512 if big_k else _pick(n_cols, (1024, 512, 256))
    tm = _pick(m_rows, (512, 384, 256)) if big_k else _row_tile(m_rows)

    def epi(accs, m, h_ref, gate_ref):
        row = m * tm + lax.broadcasted_iota(I32, (tm, 1), 0)
        g = jnp.where(row < n_ctx, gate_ref[0:1, :], gate_ref[1:2, :])
        return (h_ref[...] + g * accs[0],)

    return _mm(x, w, sel=sel, col_offs=(0,), n_tiles=n_cols // tn, tn=tn, tm=tm, epi=epi,
               extra=(h, gate),
               extra_specs=(pl.BlockSpec((tm, tn), lambda n, m: (m, n)),
                            pl.BlockSpec((2, tn), lambda n, m: (0, n))),
               out_shape=[jax.ShapeDtypeStruct((m_rows, n_cols), F32)],
               out_specs=[pl.BlockSpec((tm, tn), lambda n, m: (m, n))], name=name)[0]


def _proj_moe_down(x, w, group, row_w):
    m_rows, k = x.shape
    n_cols = w.shape[2]
    tn, tm = 512, 512
    return _mm(x, w, sel=None, col_offs=(0,), n_tiles=n_cols // tn, tn=tn, tm=tm,
               epi=lambda accs, m, rw_ref: (accs[0] * rw_ref[...],),
               extra=(row_w,), extra_specs=(pl.BlockSpec((tm, 1), lambda n, m, *pf: (m, 0)),),
               out_shape=[jax.ShapeDtypeStruct((m_rows, n_cols), F32)],
               out_specs=[pl.BlockSpec((tm, tn), lambda n, m, *pf: (m, n))],
               group=group, name="moe_down")[0]


def _conv_gate_kernel(b_ref, u_ref, up_ref, un_ref, cw_ref, z_ref, *, tm, n_ctx_blocks, n_blocks):
    i = pl.program_id(0)
    u = u_ref[...].astype(F32)
    prev_ok = jnp.logical_and(i != 0, i != n_ctx_blocks).astype(F32)
    next_ok = jnp.logical_and(i != n_ctx_blocks - 1, i != n_blocks - 1).astype(F32)
    prow = up_ref[15:16, :].astype(F32) * prev_ok
    nrow = un_ref[0:1, :].astype(F32) * next_ok
    row = lax.broadcasted_iota(I32, (tm, 1), 0)
    um1 = jnp.where(row == 0, prow, pltpu.roll(u, 1, 0))
    up1 = jnp.where(row == tm - 1, nrow, pltpu.roll(u, tm - 1, 0))
    conv = cw_ref[0:1, :] * um1 + cw_ref[1:2, :] * u + cw_ref[2:3, :] * up1
    z_ref[...] = (b_ref[...].astype(F32) * conv).astype(z_ref.dtype)


def _conv_gate(b, u, conv_w, n_ctx):
    m, d = u.shape
    tm = ROW_BLOCK
    tc = _pick(d, (512, 256, 128))
    hb = 16
    nb = m // tm
    last_hb = m // hb - 1
    kern = functools.partial(_conv_gate_kernel, tm=tm, n_ctx_blocks=n_ctx // tm, n_blocks=nb)
    return pl.pallas_call(
        kern,
        grid=(nb, d // tc),
        in_specs=[pl.BlockSpec((tm, tc), lambda i, j: (i, j)),
                  pl.BlockSpec((tm, tc), lambda i, j: (i, j)),
                  pl.BlockSpec((hb, tc), lambda i, j: (jnp.maximum(i * (tm // hb) - 1, 0), j)),
                  pl.BlockSpec((hb, tc), lambda i, j: (jnp.minimum((i + 1) * (tm // hb), last_hb), j)),
                  pl.BlockSpec((3, tc), lambda i, j: (0, j))],
        out_specs=pl.BlockSpec((tm, tc), lambda i, j: (i, j)),
        out_shape=jax.ShapeDtypeStruct((m, d), BF16),
        compiler_params=_cparams("arbitrary", "arbitrary"),
        name="conv_gate",
    )(b, u, u, u, conv_w)


def _na_bias_tables(rpb, rows):
    nb = rows // NA_QROWS
    r = np.arange(NA_QROWS)[:, None, None, None]
    c = np.arange(GRID_W)[None, :, None, None]
    j = np.arange(NA_WROWS)[None, None, :, None]
    kc = np.arange(GRID_W)[None, None, None, :]
    c0 = np.clip(c - NA_KW // 2, 0, GRID_W - NA_KW)
    tabs = [jnp.full((rpb.shape[0], NA_QROWS * GRID_W, NA_WROWS * GRID_W), MASKED, F32)]
    for b in (0, 1, nb - 1):
        w0 = NA_QROWS * int(np.clip(b - 1, 0, nb - 3))
        ar = NA_QROWS * b + r
        r0 = np.clip(ar - NA_KH // 2, 0, rows - NA_KH)
        kr = w0 + j
        valid = (kr >= r0) & (kr < r0 + NA_KH) & (kc >= c0) & (kc < c0 + NA_KW)
        dr = np.clip(kr - ar + NA_KH - 1, 0, 2 * NA_KH - 2)
        dc = np.clip(kc - c + NA_KW - 1, 0, 2 * NA_KW - 2)
        shape = (NA_QROWS * GRID_W, NA_WROWS * GRID_W)
        full = (NA_QROWS, GRID_W, NA_WROWS, GRID_W)
        dr = np.broadcast_to(dr, full).reshape(shape)
        dc = np.broadcast_to(dc, full).reshape(shape)
        valid = np.broadcast_to(valid, full).reshape(shape)
        tabs.append(jnp.where(valid[None], rpb[:, dr, dc], MASKED))
    return jnp.stack(tabs)


def _na_kernel(q_ref, k0_ref, k1_ref, k2_ref, kc_ref, v0_ref, v1_ref, v2_ref, vc_ref, bias_ref, o_ref, *, scale):
    blk = ROW_BLOCK
    for h in range(NA_HEADS):
        hs = slice(h * HEAD_DIM, (h + 1) * HEAD_DIM)
        q = q_ref[:, hs]
        ss = [_dot_nt(q, kr[:, hs]) * scale + bias_ref[h, :, j * blk:(j + 1) * blk]
              for j, kr in enumerate((k0_ref, k1_ref, k2_ref))]
        ss.append(_dot_nt(q, kc_ref[:, hs]) * scale)
        mx = functools.reduce(jnp.maximum, [jnp.max(s, axis=-1, keepdims=True) for s in ss])
        ps = [jnp.exp(s - mx) for s in ss]
        den = functools.reduce(jnp.add, [jnp.sum(p, axis=-1, keepdims=True) for p in ps])
        acc = functools.reduce(jnp.add, [_dot(p.astype(BF16), vr[:, hs])
                                         for p, vr in zip(ps, (v0_ref, v1_ref, v2_ref, vc_ref))])
        o_ref[:, hs] = (acc / den).astype(o_ref.dtype)


def _na_attention(qkv, bias, n_ctx):
    m, d3 = qkv.shape
    d = d3 // 3
    blk = ROW_BLOCK
    assert n_ctx == blk and NA_QROWS * GRID_W == blk
    nq = m // blk
    nb = nq - 1

    def kv_map(j, col):
        return lambda g: (1 + jnp.clip(g - 2, 0, nb - 3) + j, col)

    def bias_map(g):
        return (jnp.where(g == 0, 0, jnp.where(g == 1, 1, jnp.where(g == nq - 1, 3, 2))), 0, 0, 0)

    blkspec = lambda imap: pl.BlockSpec((blk, d), imap)
    return pl.pallas_call(
        functools.partial(_na_kernel, scale=HEAD_DIM ** -0.5),
        grid=(nq,),
        in_specs=[blkspec(lambda g: (g, 0)),
                  blkspec(kv_map(0, 1)), blkspec(kv_map(1, 1)), blkspec(kv_map(2, 1)), blkspec(lambda g: (0, 1)),
                  blkspec(kv_map(0, 2)), blkspec(kv_map(1, 2)), blkspec(kv_map(2, 2)), blkspec(lambda g: (0, 2)),
                  pl.BlockSpec((None, NA_HEADS, blk, 3 * blk), bias_map, pipeline_mode=pl.Buffered(1))],
        out_specs=blkspec(lambda g: (g, 0)),
        out_shape=jax.ShapeDtypeStruct((m, d), BF16),
        compiler_params=_cparams("arbitrary"),
        name="na_attention",
    )(qkv, qkv, qkv, qkv, qkv, qkv, qkv, qkv, qkv, bias)


def _rope_tables(n_ctx, n_lat):
    t = jnp.arange(n_lat)
    row = (t // GRID_W).astype(F32)
    col = (t % GRID_W).astype(F32)
    quarter = HEAD_DIM // 4
    inv_freq = ROPE_THETA ** (-jnp.arange(quarter, dtype=F32) / quarter)
    ar, ac = row[:, None] * inv_freq, col[:, None] * inv_freq
    cos_t = jnp.concatenate([jnp.cos(ar), jnp.cos(ar), jnp.cos(ac), jnp.cos(ac)], axis=1)
    sin_t = jnp.concatenate([-jnp.sin(ar), jnp.sin(ar), -jnp.sin(ac), jnp.sin(ac)], axis=1)
    cos_t = jnp.concatenate([jnp.ones((n_ctx, HEAD_DIM), F32), cos_t], axis=0)
    sin_t = jnp.concatenate([jnp.zeros((n_ctx, HEAD_DIM), F32), sin_t], axis=0)
    return cos_t, sin_t


def _gqa_kernel(q_ref, k_ref, v_ref, o_ref, *, n_ctx, tk, n_lat_chunks, group):
    is_ctx = pl.program_id(1) * ROW_BLOCK < n_ctx
    n_it = jnp.where(is_ctx, 0, n_lat_chunks)
    for g in range(group):
        hs = slice(g * HEAD_DIM, (g + 1) * HEAD_DIM)
        q = q_ref[:, hs]
        s = _dot_nt(q, k_ref[0:n_ctx, :])
        mx = jnp.max(s, axis=-1, keepdims=True)
        p = jnp.exp(s - mx)
        den = jnp.sum(p, axis=-1, keepdims=True)
        acc = _dot(p.astype(BF16), v_ref[0:n_ctx, :])

        def body(c, carry):
            mx, den, acc = carry
            off = pl.multiple_of(n_ctx + c * tk, ROW_BLOCK)
            s = _dot_nt(q, k_ref[pl.ds(off, tk), :])
            mx_new = jnp.maximum(mx, jnp.max(s, axis=-1, keepdims=True))
            alpha = jnp.exp(mx - mx_new)
            p = jnp.exp(s - mx_new)
            den = alpha * den + jnp.sum(p, axis=-1, keepdims=True)
            acc = alpha * acc + _dot(p.astype(BF16), v_ref[pl.ds(off, tk), :])
            return mx_new, den, acc

        mx, den, acc = lax.fori_loop(0, n_it, body, (mx, den, acc))
        o_ref[:, hs] = (acc / den).astype(o_ref.dtype)


def _gqa_attention(q, k, v, n_ctx):
    m, dq = q.shape
    group = GQA_HEADS // GQA_KV_HEADS
    gw = group * HEAD_DIM
    tq = ROW_BLOCK
    n_lat = m - n_ctx
    tk = _pick(n_lat, (1024, 512, 256))
    kern = functools.partial(_gqa_kernel, n_ctx=n_ctx, tk=tk, n_lat_chunks=n_lat // tk, group=group)
    return pl.pallas_call(
        kern,
        grid=(GQA_KV_HEADS, m // tq),
        in_specs=[pl.BlockSpec((tq, gw), lambda kh, i: (i, kh)),
                  pl.BlockSpec((m, HEAD_DIM), lambda kh, i: (0, kh)),
                  pl.BlockSpec((m, HEAD_DIM), lambda kh, i: (0, kh))],
        out_specs=pl.BlockSpec((tq, gw), lambda kh, i: (i, kh)),
        out_shape=jax.ShapeDtypeStruct((m, dq), BF16),
        compiler_params=_cparams("arbitrary", "arbitrary"),
        name="gqa_attention",
    )(q, k, v)


MOE_TILE = 512


def _moe_plan(ids, wts):
    m = ids.shape[0]
    tm = MOE_TILE
    n_rows = -(-(TOP_K * m + N_EXPERTS * (tm - 1)) // tm) * tm
    flat_e = ids.reshape(-1)
    onehot = (flat_e[:, None] == jnp.arange(N_EXPERTS, dtype=I32)[None, :]).astype(I32)
    csum = jnp.cumsum(onehot, axis=0)
    rank = jnp.sum(csum * onehot, axis=1) - 1
    counts = csum[-1]
    padded = ((counts + tm - 1) // tm) * tm
    ends = jnp.cumsum(padded)
    starts = ends - padded
    pos = jnp.sum(starts[None, :] * onehot, axis=1) + rank
    n_used = (ends[-1] // tm).astype(I32).reshape(1)
    tile_start = jnp.arange(n_rows // tm, dtype=I32) * tm
    tile_expert = jnp.minimum(jnp.sum((tile_start[:, None] >= ends[None, :]).astype(I32), axis=1), N_EXPERTS - 1)
    tok = jnp.arange(TOP_K * m, dtype=I32) // TOP_K
    src = jnp.zeros((n_rows,), I32).at[pos].set(tok)
    row_w = jnp.zeros((n_rows,), F32).at[pos].set(wts.reshape(-1))
    return src, row_w.reshape(n_rows, 1), pos.astype(I32), tile_expert.astype(I32), n_used


def _gather_norm_kernel(src_ref, tok_ref, h_hbm, g_ref, ssc_ref, ssl_ref, o_ref, buf, sem, *, tg, n_ctx):
    i = pl.program_id(0)
    base = i * tg

    def row_copy(r, row):
        return pltpu.make_async_copy(h_hbm.at[pl.ds(row, 1), :], buf.at[pl.ds(r, 1), :], sem)

    def start(r, carry):
        row_copy(r, src_ref[base + r]).start()
        return carry

    def wait(r, carry):
        row_copy(r, 0).wait()
        return carry

    lax.fori_loop(0, tg, start, 0)
    lax.fori_loop(0, tg, wait, 0)
    y = _rms(buf[...], g_ref[...])
    is_ctx = tok_ref[...] < n_ctx
    scale = jnp.where(is_ctx, ssc_ref[1:2, :], ssl_ref[1:2, :])
    shift = jnp.where(is_ctx, ssc_ref[0:1, :], ssl_ref[0:1, :])
    o_ref[...] = (y * (1.0 + scale) + shift).astype(o_ref.dtype)


def _gather_norm(h, src, gain, ss, n_ctx):
    m, d = h.shape
    n_rows = src.shape[0]
    tg = ROW_BLOCK
    kern = functools.partial(_gather_norm_kernel, tg=tg, n_ctx=n_ctx)
    gs = pltpu.PrefetchScalarGridSpec(
        num_scalar_prefetch=1,
        grid=(n_rows // tg,),
        in_specs=[pl.BlockSpec((tg, 1), lambda i, *pf: (i, 0)),
                  pl.BlockSpec(memory_space=pl.ANY),
                  pl.BlockSpec((1, d), lambda i, *pf: (0, 0)),
                  pl.BlockSpec((None, 2, d), lambda i, *pf: (0, 0, 0)),
                  pl.BlockSpec((None, 2, d), lambda i, *pf: (1, 0, 0))],
        out_specs=pl.BlockSpec((tg, d), lambda i, *pf: (i, 0)),
        scratch_shapes=[pltpu.VMEM((tg, d), F32), pltpu.SemaphoreType.DMA(())])
    return pl.pallas_call(kern, grid_spec=gs, out_shape=jax.ShapeDtypeStruct((n_rows, d), BF16),
                          compiler_params=_cparams("arbitrary"),
                          name="moe_gather")(src, src.reshape(n_rows, 1), h, gain.reshape(1, d), ss, ss)


def _combine_kernel(pos_ref, h_ref, gate_ref, y_hbm, o_ref, buf, sem, *, tg, n_ctx):
    i = pl.program_id(0)
    base = i * tg * TOP_K

    def row_copy(r, row):
        return pltpu.make_async_copy(y_hbm.at[pl.ds(row, 1), :], buf.at[pl.ds(r, 1), :], sem)

    def start(r, carry):
        row_copy(r, pos_ref[base + r]).start()
        return carry

    def wait(r, carry):
        row_copy(r, 0).wait()
        return carry

    lax.fori_loop(0, tg * TOP_K, start, 0)
    lax.fori_loop(0, tg * TOP_K, wait, 0)
    row = i * tg + lax.broadcasted_iota(I32, (tg, 1), 0)
    g = jnp.where(row < n_ctx, gate_ref[0:1, :], gate_ref[1:2, :])
    o_ref[...] = h_ref[...] + g * (buf[0:tg, :] + buf[tg:TOP_K * tg, :])


def _combine(h, y, pos, gate, n_ctx):
    m, d = h.shape
    tg = ROW_BLOCK
    pos_tiles = pos.reshape(m // tg, tg, TOP_K).transpose(0, 2, 1).reshape(-1)
    kern = functools.partial(_combine_kernel, tg=tg, n_ctx=n_ctx)
    gs = pltpu.PrefetchScalarGridSpec(
        num_scalar_prefetch=1,
        grid=(m // tg,),
        in_specs=[pl.BlockSpec((tg, d), lambda i, *pf: (i, 0)),
                  pl.BlockSpec((2, d), lambda i, *pf: (0, 0)),
                  pl.BlockSpec(memory_space=pl.ANY)],
        out_specs=pl.BlockSpec((tg, d), lambda i, *pf: (i, 0)),
        scratch_shapes=[pltpu.VMEM((TOP_K * tg, d), F32), pltpu.SemaphoreType.DMA(())])
    return pl.pallas_call(kern, grid_spec=gs, out_shape=jax.ShapeDtypeStruct((m, d), F32),
                          compiler_params=_cparams("arbitrary"), name="moe_combine")(pos_tiles, h, gate, y)


def _moe(h, gain, ss, gate, w_router, w_gu, w_down, j, n_ctx):
    ids, wts = _router(h, gain, ss, w_router[j], n_ctx)
    src, row_w, pos, tile_expert, n_used = _moe_plan(ids[:, :TOP_K], wts[:, :TOP_K])
    xs = _gather_norm(h, src, gain, ss, n_ctx)
    group = (tile_expert + j * N_EXPERTS, n_used)
    act = _proj_gu(xs, w_gu.reshape((-1,) + w_gu.shape[2:]), None, group=group)
    y = _proj_moe_down(act, w_down.reshape((-1,) + w_down.shape[2:]), group, row_w)
    return _combine(h, y, pos, gate, n_ctx)


def kernel(x, c, ctx, c_ctx, ada_w, ada_b, norm_mix, norm_ffn, norm_final, conv_w_in, conv_w, conv_w_out,
           na_w_qkv, na_rpb, na_w_out, gqa_w_q, gqa_w_kv, gqa_q_norm, gqa_k_norm, gqa_w_out, ffn_w_gu,
           ffn_w_down, moe_w_router, moe_w_gu, moe_w_down):
    bsz, n_lat, d = x.shape
    n_ctx_full = ctx.shape[1]
    depth = ada_w.shape[0]
    assert bsz == 1 and n_ctx_full == ROW_BLOCK and n_lat % (GRID_W * NA_QROWS) == 0
    assert n_lat // GRID_W >= NA_WROWS

    cvec = jnp.zeros((8, d), F32).at[0].set(c_ctx).at[1].set(c[0])
    mod = _ada(cvec, ada_w, ada_b)[:, :2].reshape(depth, 2, 6, d)

    h = jnp.concatenate([ctx[0], x[0]], axis=0)
    n_ctx = n_ctx_full
    cos_t, sin_t = _rope_tables(n_ctx_full, n_lat)

    for i in range(depth):
        if i == depth - 1:
            h = h[n_ctx:]
            n_ctx = 0
        kind, j = i % N_MIXERS, i // N_MIXERS
        a = _norm_mod(h, norm_mix[i], mod[i, :, 0:2], n_ctx)
        gate1 = mod[i, :, 2]
        if kind == 0:
            b_gate, u = _proj_conv_in(a, conv_w_in, j)
            z = _conv_gate(b_gate, u, conv_w[j], n_ctx)
            h = _proj_resid(z, conv_w_out, j, h, gate1, n_ctx, "conv_out")
        elif kind == 1:
            qkv = _proj_plain(a, na_w_qkv, j, 0, 3 * d, "na_qkv")
            bias = _na_bias_tables(na_rpb[j], n_lat // GRID_W)
            o = _na_attention(qkv, bias, n_ctx)
            h = _proj_resid(o, na_w_out, j, h, gate1, n_ctx, "na_out")
        else:
            dq = GQA_HEADS * HEAD_DIM
            dkv = GQA_KV_HEADS * HEAD_DIM
            q = _proj_qk_rope(a, gqa_w_q, j, 0, dq, gqa_q_norm[j], cos_t, sin_t, HEAD_DIM ** -0.5, "gqa_q")
            k = _proj_qk_rope(a, gqa_w_kv, j, 0, dkv, gqa_k_norm[j], cos_t, sin_t, 1.0, "gqa_k")
            v = _proj_plain(a, gqa_w_kv, j, dkv, dkv, "gqa_v")
            o = _gqa_attention(q, k, v, n_ctx)
            h = _proj_resid(o, gqa_w_out, j, h, gate1, n_ctx, "gqa_out")
        gate2 = mod[i, :, 5]
        if i % 2 == 0:
            f = _norm_mod(h, norm_ffn[i], mod[i, :, 3:5], n_ctx)
            act = _proj_gu(f, ffn_w_gu, i // 2)
            h = _proj_resid(act, ffn_w_down, i // 2, h, gate2, n_ctx, "ffn_down")
        else:
            h = _moe(h, norm_ffn[i], mod[i, :, 3:5], gate2, moe_w_router, moe_w_gu, moe_w_down, i // 2, n_ctx)
    return _final_norm(h, norm_final)[None]
```

```python
import functools

import numpy as np
import jax
import jax.numpy as jnp
from jax import lax
from jax.experimental import pallas as pl
from jax.experimental.pallas import tpu as pltpu

F32 = jnp.float32
BF16 = jnp.bfloat16
I32 = jnp.int32

EPS = 1e-6
GRID_W = 64
N_MIXERS = 3
NA_HEADS = 16
NA_KH = 8
NA_KW = 16
NA_QROWS = 4
NA_WROWS = NA_QROWS + NA_KH
GQA_HEADS = 16
GQA_KV_HEADS = 4
HEAD_DIM = 128
ROPE_THETA = 10000.0
N_EXPERTS = 8
TOP_K = 2
LANES = 128
ROW_BLOCK = 256
MASKED = -1e30
LOG2_E = float(np.log2(np.e))
VMEM_LIMIT = 60 * 1024 * 1024


def _cparams(*sem):
    return pltpu.CompilerParams(dimension_semantics=sem, vmem_limit_bytes=VMEM_LIMIT)


def _pick(n, cands):
    for c in cands:
        if n % c == 0:
            return c
    raise ValueError(f"no tile for {n} in {cands}")


def _dot(a, b):
    return jnp.dot(a, b, preferred_element_type=F32)


def _dot_nt(a, b):
    return lax.dot_general(a, b, (((1,), (1,)), ((), ())), preferred_element_type=F32)


def _silu(x):
    return x * (1.0 / (1.0 + jnp.exp(-x)))


def _rms(x, gain):
    return x * lax.rsqrt(jnp.mean(x * x, axis=-1, keepdims=True) + EPS) * gain


def _ada_kernel(c_ref, w_ref, b_ref, o_ref):
    s = _silu(c_ref[...])
    o_ref[...] = jnp.dot(s, w_ref[...], precision=lax.Precision.HIGHEST,
                         preferred_element_type=F32) + b_ref[...]


def _ada(cvec, ada_w, ada_b):
    depth, d, n = ada_w.shape
    tn = _pick(n, (1024, 512, 256, 128))
    return pl.pallas_call(
        _ada_kernel,
        grid=(depth, n // tn),
        in_specs=[pl.BlockSpec((8, d), lambda i, j: (0, 0)),
                  pl.BlockSpec((None, d, tn), lambda i, j: (i, 0, j)),
                  pl.BlockSpec((None, 1, tn), lambda i, j: (i, 0, j))],
        out_specs=pl.BlockSpec((None, 8, tn), lambda i, j: (i, 0, j)),
        out_shape=jax.ShapeDtypeStruct((depth, 8, n), F32),
        compiler_params=_cparams("arbitrary", "arbitrary"),
        name="ada",
    )(cvec, ada_w, ada_b.reshape(depth, 1, n))


def _norm_mod_kernel(h_ref, g_ref, ss_ref, a_ref):
    y = _rms(h_ref[...], g_ref[...])
    a_ref[...] = (y * (1.0 + ss_ref[1:2, :]) + ss_ref[0:1, :]).astype(a_ref.dtype)


def _seg_map(n_ctx_blocks):
    return lambda i: (jnp.where(i < n_ctx_blocks, 0, 1), 0, 0)


def _norm_mod(h, gain, ss, n_ctx):
    m, d = h.shape
    tm = ROW_BLOCK
    return pl.pallas_call(
        _norm_mod_kernel,
        grid=(m // tm,),
        in_specs=[pl.BlockSpec((tm, d), lambda i: (i, 0)),
                  pl.BlockSpec((1, d), lambda i: (0, 0)),
                  pl.BlockSpec((None, 2, d), _seg_map(n_ctx // tm))],
        out_specs=pl.BlockSpec((tm, d), lambda i: (i, 0)),
        out_shape=jax.ShapeDtypeStruct((m, d), BF16),
        compiler_params=_cparams("arbitrary"),
        name="norm_mod",
    )(h, gain.reshape(1, d), ss)


def _router_kernel(h_ref, g_ref, ss_ref, wr_ref, ids_ref, wts_ref):
    y = _rms(h_ref[...], g_ref[...])
    f = y * (1.0 + ss_ref[1:2, :]) + ss_ref[0:1, :]
    logits = jnp.dot(f, wr_ref[...], precision=lax.Precision.HIGHEST, preferred_element_type=F32)
    lane = lax.broadcasted_iota(I32, logits.shape, 1)
    neg = jnp.float32(-jnp.inf)
    l1 = jnp.where(lane < N_EXPERTS, logits, neg)
    m1 = jnp.max(l1, axis=-1, keepdims=True)
    i1 = jnp.min(jnp.where(l1 == m1, lane, LANES), axis=-1, keepdims=True)
    l2 = jnp.where(lane == i1, neg, l1)
    m2 = jnp.max(l2, axis=-1, keepdims=True)
    i2 = jnp.min(jnp.where(l2 == m2, lane, LANES), axis=-1, keepdims=True)
    e2 = jnp.exp(m2 - m1)
    w1 = 1.0 / (1.0 + e2)
    w2 = e2 / (1.0 + e2)
    ids_ref[...] = jnp.where(lane == 0, i1, jnp.where(lane == 1, i2, 0))
    wts_ref[...] = jnp.where(lane == 0, w1, jnp.where(lane == 1, w2, 0.0))


def _router(h, gain, ss, w_router, n_ctx):
    m, d = h.shape
    tm = ROW_BLOCK
    wr = jnp.zeros((d, LANES), F32).at[:, :N_EXPERTS].set(w_router)
    return pl.pallas_call(
        _router_kernel,
        grid=(m // tm,),
        in_specs=[pl.BlockSpec((tm, d), lambda i: (i, 0)),
                  pl.BlockSpec((1, d), lambda i: (0, 0)),
                  pl.BlockSpec((None, 2, d), _seg_map(n_ctx // tm)),
                  pl.BlockSpec((d, LANES), lambda i: (0, 0))],
        out_specs=[pl.BlockSpec((tm, LANES), lambda i: (i, 0)),
                   pl.BlockSpec((tm, LANES), lambda i: (i, 0))],
        out_shape=[jax.ShapeDtypeStruct((m, LANES), I32),
                   jax.ShapeDtypeStruct((m, LANES), F32)],
        compiler_params=_cparams("arbitrary"),
        name="router",
    )(h, gain.reshape(1, d), ss, wr)


def _final_norm_kernel(h_ref, g_ref, o_ref):
    o_ref[...] = _rms(h_ref[...], g_ref[...])


def _final_norm(h, gain):
    m, d = h.shape
    tm = ROW_BLOCK
    return pl.pallas_call(
        _final_norm_kernel,
        grid=(m // tm,),
        in_specs=[pl.BlockSpec((tm, d), lambda i: (i, 0)),
                  pl.BlockSpec((1, d), lambda i: (0, 0))],
        out_specs=pl.BlockSpec((tm, d), lambda i: (i, 0)),
        out_shape=jax.ShapeDtypeStruct((m, d), F32),
        compiler_params=_cparams("arbitrary"),
        name="final_norm",
    )(h, gain.reshape(1, d))


def _mm(x, w, *, sel, col_offs, n_tiles, tn, tm, epi, extra=(), extra_specs=(), out_shape, out_specs,
        group=None, w_single_buffer=False, name):
    m_rows, k = x.shape
    n_w = len(col_offs)
    n_pf = 0 if group is None else 2
    n_ex = len(extra)
    n_out = len(out_shape)

    def kern(*refs):
        pf = refs[:n_pf]
        x_ref = refs[n_pf]
        w_refs = refs[n_pf + 1:n_pf + 1 + n_w]
        ex = refs[n_pf + 1 + n_w:n_pf + 1 + n_w + n_ex]
        outs = refs[n_pf + 1 + n_w + n_ex:n_pf + 1 + n_w + n_ex + n_out]
        wb = refs[-1]
        m = pl.program_id(1)
        if group is None:
            is_new = m == 0
        else:
            te, nu = pf
            is_new = jnp.logical_or(m == 0, te[m] != te[jnp.maximum(m - 1, 0)])

        @pl.when(is_new)
        def _():
            for j in range(n_w):
                wb[j] = w_refs[j][...].astype(BF16)

        def compute():
            xv = x_ref[...]
            accs = [_dot(xv, wb[j]) for j in range(n_w)]
            for o, r in zip(outs, epi(accs, m, *ex)):
                o[...] = r.astype(o.dtype)

        if group is None:
            compute()
        else:
            active = m < nu[0]
            pl.when(active)(compute)

            @pl.when(jnp.logical_not(active))
            def _():
                for o in outs:
                    o[...] = jnp.zeros(o.shape, o.dtype)

    def w_map(off):
        if group is None:
            return lambda n, m: (sel, 0, n + off)
        return lambda n, m, te, nu: (te[m], 0, n + off)

    w_kw = {"pipeline_mode": pl.Buffered(1)} if w_single_buffer else {}
    in_specs = ([pl.BlockSpec((tm, k), lambda n, m, *pf: (m, 0))]
                + [pl.BlockSpec((None, k, tn), w_map(off), **w_kw) for off in col_offs]
                + list(extra_specs))
    grid = (n_tiles, m_rows // tm)
    scratch = [pltpu.VMEM((n_w, k, tn), BF16)]
    cp = _cparams("arbitrary", "arbitrary")
    if group is None:
        return pl.pallas_call(kern, grid=grid, in_specs=in_specs, out_specs=out_specs, out_shape=out_shape,
                              scratch_shapes=scratch, compiler_params=cp, name=name)(x, *([w] * n_w), *extra)
    gs = pltpu.PrefetchScalarGridSpec(num_scalar_prefetch=2, grid=grid, in_specs=in_specs,
                                      out_specs=out_specs, scratch_shapes=scratch)
    return pl.pallas_call(kern, grid_spec=gs, out_shape=out_shape, compiler_params=cp,
                          name=name)(group[0], group[1], x, *([w] * n_w), *extra)


def _row_tile(m_rows):
    return _pick(m_rows, (768, 512, 256))


def _proj_plain(x, w, sel, col0, n_cols, name):
    m_rows = x.shape[0]
    tn = _pick(n_cols, (1024, 512, 256))
    assert col0 % tn == 0
    tm = _row_tile(m_rows)
    return _mm(x, w, sel=sel, col_offs=(col0 // tn,), n_tiles=n_cols // tn, tn=tn, tm=tm,
               epi=lambda accs, m: (accs[0],),
               out_shape=[jax.ShapeDtypeStruct((m_rows, n_cols), BF16)],
               out_specs=[pl.BlockSpec((tm, tn), lambda n, m: (m, n))], name=name)[0]


def _proj_conv_in(x, w, sel):
    m_rows = x.shape[0]
    d3 = w.shape[2] // 3
    tn = _pick(d3, (512, 256))
    tm = _row_tile(m_rows)
    nt = d3 // tn
    o = jax.ShapeDtypeStruct((m_rows, d3), BF16)
    spec = pl.BlockSpec((tm, tn), lambda n, m: (m, n))
    return _mm(x, w, sel=sel, col_offs=(0, nt, 2 * nt), n_tiles=nt, tn=tn, tm=tm,
               epi=lambda accs, m: (accs[0], accs[1] * accs[2]),
               out_shape=[o, o], out_specs=[spec, spec], name="conv_in")


def _proj_gu(x, w, sel, group=None):
    m_rows = x.shape[0]
    f = w.shape[2] // 2
    tn = _pick(f, (512, 256))
    tm = 512 if group is not None else _row_tile(m_rows)
    nt = f // tn
    return _mm(x, w, sel=sel, col_offs=(0, nt), n_tiles=nt, tn=tn, tm=tm,
               epi=lambda accs, m: (_silu(accs[0]) * accs[1],),
               out_shape=[jax.ShapeDtypeStruct((m_rows, f), BF16)],
               out_specs=[pl.BlockSpec((tm, tn), lambda n, m, *pf: (m, n))],
               group=group, name="ffn_gu" if group is None else "moe_gu")[0]


def _proj_qk_rope(x, w, sel, col0, n_cols, gain, cos_t, sin_t, scale, name):
    m_rows = x.shape[0]
    tn = _pick(n_cols, (512, 256, 128))
    assert col0 % tn == 0
    tm = _row_tile(m_rows)

    def epi(accs, m, gain_ref, cos_ref, sin_ref):
        lane = lax.broadcasted_iota(I32, (1, HEAD_DIM), 1)
        first = (lane & (HEAD_DIM // 4)) == 0
        cos_v, sin_v, g = cos_ref[...], sin_ref[...], gain_ref[...]
        heads = []
        for hh in range(tn // HEAD_DIM):
            y = _rms(accs[0][:, hh * HEAD_DIM:(hh + 1) * HEAD_DIM], g)
            rot = jnp.where(first, pltpu.roll(y, HEAD_DIM - HEAD_DIM // 4, 1), pltpu.roll(y, HEAD_DIM // 4, 1))
            heads.append((y * cos_v + rot * sin_v) * scale)
        return (jnp.concatenate(heads, axis=1),)

    return _mm(x, w, sel=sel, col_offs=(col0 // tn,), n_tiles=n_cols // tn, tn=tn, tm=tm, epi=epi,
               extra=(gain.reshape(1, HEAD_DIM), cos_t, sin_t),
               extra_specs=(pl.BlockSpec((1, HEAD_DIM), lambda n, m: (0, 0)),
                            pl.BlockSpec((tm, HEAD_DIM), lambda n, m: (m, 0)),
                            pl.BlockSpec((tm, HEAD_DIM), lambda n, m: (m, 0))),
               out_shape=[jax.ShapeDtypeStruct((m_rows, n_cols), BF16)],
               out_specs=[pl.BlockSpec((tm, tn), lambda n, m: (m, n))], name=name)[0]


def _proj_resid(x, w, sel, h, gate, n_ctx, name):
    m_rows, k = x.shape
    n_cols = w.shape[2]
    big_k = k > 4096
    tn = 512 if big_k else _pick(n_cols, (1024, 512, 256))
    tm = _pick(m_rows, (512, 384, 256)) if big_k else _row_tile(m_rows)

    def epi(accs, m, h_ref, gate_ref):
        row = m * tm + lax.broadcasted_iota(I32, (tm, 1), 0)
        g = jnp.where(row < n_ctx, gate_ref[0:1, :], gate_ref[1:2, :])
        return (h_ref[...] + g * accs[0],)

    return _mm(x, w, sel=sel, col_offs=(0,), n_tiles=n_cols // tn, tn=tn, tm=tm, epi=epi,
               extra=(h, gate),
               extra_specs=(pl.BlockSpec((tm, tn), lambda n, m: (m, n)),
                            pl.BlockSpec((2, tn), lambda n, m: (0, n))),
               out_shape=[jax.ShapeDtypeStruct((m_rows, n_cols), F32)],
               out_specs=[pl.BlockSpec((tm, tn), lambda n, m: (m, n))], name=name)[0]


def _proj_moe_down(x, w, group, row_w):
    m_rows, k = x.shape
    n_cols = w.shape[2]
    tn, tm = 512, 512
    return _mm(x, w, sel=None, col_offs=(0,), n_tiles=n_cols // tn, tn=tn, tm=tm,
               epi=lambda accs, m, rw_ref: (accs[0] * rw_ref[...],),
               extra=(row_w,), extra_specs=(pl.BlockSpec((tm, 1), lambda n, m, *pf: (m, 0)),),
               out_shape=[jax.ShapeDtypeStruct((m_rows, n_cols), F32)],
               out_specs=[pl.BlockSpec((tm, tn), lambda n, m, *pf: (m, n))],
               group=group, name="moe_down")[0]


def _conv_gate_kernel(b_ref, u_ref, up_ref, un_ref, cw_ref, z_ref, *, tm, n_ctx_blocks, n_blocks):
    i = pl.program_id(0)
    u = u_ref[...].astype(F32)
    prev_ok = jnp.logical_and(i != 0, i != n_ctx_blocks).astype(F32)
    next_ok = jnp.logical_and(i != n_ctx_blocks - 1, i != n_blocks - 1).astype(F32)
    prow = up_ref[15:16, :].astype(F32) * prev_ok
    nrow = un_ref[0:1, :].astype(F32) * next_ok
    row = lax.broadcasted_iota(I32, (tm, 1), 0)
    um1 = jnp.where(row == 0, prow, pltpu.roll(u, 1, 0))
    up1 = jnp.where(row == tm - 1, nrow, pltpu.roll(u, tm - 1, 0))
    conv = cw_ref[0:1, :] * um1 + cw_ref[1:2, :] * u + cw_ref[2:3, :] * up1
    z_ref[...] = (b_ref[...].astype(F32) * conv).astype(z_ref.dtype)


def _conv_gate(b, u, conv_w, n_ctx):
    m, d = u.shape
    tm = ROW_BLOCK
    tc = _pick(d, (512, 256, 128))
    hb = 16
    nb = m // tm
    last_hb = m // hb - 1
    kern = functools.partial(_conv_gate_kernel, tm=tm, n_ctx_blocks=n_ctx // tm, n_blocks=nb)
    return pl.pallas_call(
        kern,
        grid=(nb, d // tc),
        in_specs=[pl.BlockSpec((tm, tc), lambda i, j: (i, j)),
                  pl.BlockSpec((tm, tc), lambda i, j: (i, j)),
                  pl.BlockSpec((hb, tc), lambda i, j: (jnp.maximum(i * (tm // hb) - 1, 0), j)),
                  pl.BlockSpec((hb, tc), lambda i, j: (jnp.minimum((i + 1) * (tm // hb), last_hb), j)),
                  pl.BlockSpec((3, tc), lambda i, j: (0, j))],
        out_specs=pl.BlockSpec((tm, tc), lambda i, j: (i, j)),
        out_shape=jax.ShapeDtypeStruct((m, d), BF16),
        compiler_params=_cparams("arbitrary", "arbitrary"),
        name="conv_gate",
    )(b, u, u, u, conv_w)


def _na_bias_tables(rpb, rows):
    nb = rows // NA_QROWS
    n_h = rpb.shape[0]
    c = np.arange(GRID_W)[:, None]
    kc = np.arange(GRID_W)[None, :]
    c0 = np.clip(c - NA_KW // 2, 0, GRID_W - NA_KW)
    c_valid = (kc >= c0) & (kc < c0 + NA_KW)
    c_sel = (kc - c + NA_KW - 1)[:, :, None] == np.arange(2 * NA_KW - 1)[None, None, :]
    c_sel = (c_sel & c_valid[:, :, None]).astype(np.float32)
    r_sel, r_valid = [], []
    for b in (0, 1, nb - 1):
        w0 = NA_QROWS * int(np.clip(b - 1, 0, nb - 3))
        ar = NA_QROWS * b + np.arange(NA_QROWS)[:, None]
        kr = w0 + np.arange(NA_WROWS)[None, :]
        r0 = np.clip(ar - NA_KH // 2, 0, rows - NA_KH)
        ok = (kr >= r0) & (kr < r0 + NA_KH)
        sel = (kr - ar + NA_KH - 1)[:, :, None] == np.arange(2 * NA_KH - 1)[None, None, :]
        r_sel.append((sel & ok[:, :, None]).astype(np.float32))
        r_valid.append(ok)
    r_sel, r_valid = np.stack(r_sel), np.stack(r_valid)
    t = jnp.einsum("pijr,hrd,ckd->phicjk", jnp.asarray(r_sel), rpb, jnp.asarray(c_sel),
                   precision=lax.Precision.HIGHEST)
    valid = r_valid[:, None, :, None, :, None] & c_valid[None, None, None, :, None, :]
    t = jnp.where(valid, t, MASKED).reshape(3, n_h, NA_QROWS * GRID_W, NA_WROWS * GRID_W)
    return jnp.concatenate([jnp.full((1,) + t.shape[1:], MASKED, F32), t], axis=0)


def _na_kernel(q_ref, k0_ref, k1_ref, k2_ref, kc_ref, v0_ref, v1_ref, v2_ref, vc_ref, bias_ref, o_ref, *, scale):
    blk = ROW_BLOCK
    for h in range(NA_HEADS):
        hs = slice(h * HEAD_DIM, (h + 1) * HEAD_DIM)
        q = q_ref[:, hs]
        ss = [_dot_nt(q, kr[:, hs]) * scale + bias_ref[h, :, j * blk:(j + 1) * blk]
              for j, kr in enumerate((k0_ref, k1_ref, k2_ref))]
        ss.append(_dot_nt(q, kc_ref[:, hs]) * scale)
        mx = functools.reduce(jnp.maximum, [jnp.max(s, axis=-1, keepdims=True) for s in ss])
        ps = [jnp.exp(s - mx) for s in ss]
        den = functools.reduce(jnp.add, [jnp.sum(p, axis=-1, keepdims=True) for p in ps])
        acc = functools.reduce(jnp.add, [_dot(p.astype(BF16), vr[:, hs])
                                         for p, vr in zip(ps, (v0_ref, v1_ref, v2_ref, vc_ref))])
        o_ref[:, hs] = (acc / den).astype(o_ref.dtype)


def _na_attention(qkv, bias, n_ctx):
    m, d3 = qkv.shape
    d = d3 // 3
    blk = ROW_BLOCK
    assert n_ctx == blk and NA_QROWS * GRID_W == blk
    nq = m // blk
    nb = nq - 1

    def kv_map(j, col):
        return lambda g: (1 + jnp.clip(g - 2, 0, nb - 3) + j, col)

    def bias_map(g):
        return (jnp.where(g == 0, 0, jnp.where(g == 1, 1, jnp.where(g == nq - 1, 3, 2))), 0, 0, 0)

    blkspec = lambda imap: pl.BlockSpec((blk, d), imap)
    return pl.pallas_call(
        functools.partial(_na_kernel, scale=HEAD_DIM ** -0.5),
        grid=(nq,),
        in_specs=[blkspec(lambda g: (g, 0)),
                  blkspec(kv_map(0, 1)), blkspec(kv_map(1, 1)), blkspec(kv_map(2, 1)), blkspec(lambda g: (0, 1)),
                  blkspec(kv_map(0, 2)), blkspec(kv_map(1, 2)), blkspec(kv_map(2, 2)), blkspec(lambda g: (0, 2)),
                  pl.BlockSpec((None, NA_HEADS, blk, 3 * blk), bias_map, pipeline_mode=pl.Buffered(1))],
        out_specs=blkspec(lambda g: (g, 0)),
        out_shape=jax.ShapeDtypeStruct((m, d), BF16),
        compiler_params=_cparams("arbitrary"),
        name="na_attention",
    )(qkv, qkv, qkv, qkv, qkv, qkv, qkv, qkv, qkv, bias)


def _rope_tables(n_ctx, n_lat):
    t = jnp.arange(n_lat)
    row = (t // GRID_W).astype(F32)
    col = (t % GRID_W).astype(F32)
    quarter = HEAD_DIM // 4
    inv_freq = ROPE_THETA ** (-jnp.arange(quarter, dtype=F32) / quarter)
    ar, ac = row[:, None] * inv_freq, col[:, None] * inv_freq
    cos_t = jnp.concatenate([jnp.cos(ar), jnp.cos(ar), jnp.cos(ac), jnp.cos(ac)], axis=1)
    sin_t = jnp.concatenate([-jnp.sin(ar), jnp.sin(ar), -jnp.sin(ac), jnp.sin(ac)], axis=1)
    cos_t = jnp.concatenate([jnp.ones((n_ctx, HEAD_DIM), F32), cos_t], axis=0)
    sin_t = jnp.concatenate([jnp.zeros((n_ctx, HEAD_DIM), F32), sin_t], axis=0)
    return cos_t, sin_t


def _gqa_kernel(q_ref, k_ref, v_ref, o_ref, *, n_ctx, tk, n_lat_chunks, group):
    is_ctx = pl.program_id(1) * ROW_BLOCK < n_ctx
    n_it = jnp.where(is_ctx, 0, n_lat_chunks)
    qs = [q_ref[:, g * HEAD_DIM:(g + 1) * HEAD_DIM] for g in range(group)]
    kc, vc = k_ref[0:n_ctx, :], v_ref[0:n_ctx, :]
    state = []
    for g in range(group):
        s = _dot_nt(qs[g], kc)
        mx = jnp.max(s, axis=-1, keepdims=True)
        p = jnp.exp2(s - mx)
        state += [mx, jnp.sum(p, axis=-1, keepdims=True), _dot(p.astype(BF16), vc)]

    def body(c, carry):
        off = pl.multiple_of(n_ctx + c * tk, ROW_BLOCK)
        kk, vv = k_ref[pl.ds(off, tk), :], v_ref[pl.ds(off, tk), :]
        out = []
        for g in range(group):
            mx, den, acc = carry[3 * g:3 * g + 3]
            s = _dot_nt(qs[g], kk)
            mx_new = jnp.maximum(mx, jnp.max(s, axis=-1, keepdims=True))
            alpha = jnp.exp2(mx - mx_new)
            p = jnp.exp2(s - mx_new)
            out += [mx_new, alpha * den + jnp.sum(p, axis=-1, keepdims=True),
                    alpha * acc + _dot(p.astype(BF16), vv)]
        return tuple(out)

    state = lax.fori_loop(0, n_it, body, tuple(state))
    for g in range(group):
        mx, den, acc = state[3 * g:3 * g + 3]
        o_ref[:, g * HEAD_DIM:(g + 1) * HEAD_DIM] = (acc / den).astype(o_ref.dtype)


def _gqa_attention(q, k, v, n_ctx):
    m, dq = q.shape
    group = GQA_HEADS // GQA_KV_HEADS
    gw = group * HEAD_DIM
    tq = ROW_BLOCK
    n_lat = m - n_ctx
    tk = _pick(n_lat, (2048, 1024, 512, 256))
    kern = functools.partial(_gqa_kernel, n_ctx=n_ctx, tk=tk, n_lat_chunks=n_lat // tk, group=group)
    return pl.pallas_call(
        kern,
        grid=(GQA_KV_HEADS, m // tq),
        in_specs=[pl.BlockSpec((tq, gw), lambda kh, i: (i, kh)),
                  pl.BlockSpec((m, HEAD_DIM), lambda kh, i: (0, kh)),
                  pl.BlockSpec((m, HEAD_DIM), lambda kh, i: (0, kh))],
        out_specs=pl.BlockSpec((tq, gw), lambda kh, i: (i, kh)),
        out_shape=jax.ShapeDtypeStruct((m, dq), BF16),
        compiler_params=_cparams("arbitrary", "arbitrary"),
        name="gqa_attention",
    )(q, k, v)


MOE_TILE = 512


def _moe_plan(ids, wts):
    m = ids.shape[0]
    tm = MOE_TILE
    n_rows = -(-(TOP_K * m + N_EXPERTS * (tm - 1)) // tm) * tm
    flat_e = ids.reshape(-1)
    onehot = (flat_e[:, None] == jnp.arange(N_EXPERTS, dtype=I32)[None, :]).astype(I32)
    csum = jnp.cumsum(onehot, axis=0)
    rank = jnp.sum(csum * onehot, axis=1) - 1
    counts = csum[-1]
    padded = ((counts + tm - 1) // tm) * tm
    ends = jnp.cumsum(padded)
    starts = ends - padded
    pos = jnp.sum(starts[None, :] * onehot, axis=1) + rank
    n_used = (ends[-1] // tm).astype(I32).reshape(1)
    tile_start = jnp.arange(n_rows // tm, dtype=I32) * tm
    tile_expert = jnp.minimum(jnp.sum((tile_start[:, None] >= ends[None, :]).astype(I32), axis=1), N_EXPERTS - 1)
    tok = jnp.arange(TOP_K * m, dtype=I32) // TOP_K
    packed = jnp.stack([tok.astype(F32), wts.reshape(-1)], axis=1)
    packed = jnp.zeros((n_rows, 2), F32).at[pos].set(packed)
    src = packed[:, 0].astype(I32)
    row_w = packed[:, 1]
    return src, row_w.reshape(n_rows, 1), pos.astype(I32), tile_expert.astype(I32), n_used


def _prefetch_rows(idx_ref, src_hbm, buf, sem, i, n_steps, n_rows):
    def row_copy(slot, r, row):
        return pltpu.make_async_copy(src_hbm.at[pl.ds(row, 1), :], buf.at[slot, pl.ds(r, 1), :], sem.at[slot])

    def start_all(step):
        slot = step % 2

        def start(r, carry):
            row_copy(slot, r, idx_ref[step * n_rows + r]).start()
            return carry

        lax.fori_loop(0, n_rows, start, 0, unroll=8)

    @pl.when(i == 0)
    def _():
        start_all(i)

    @pl.when(i + 1 < n_steps)
    def _():
        start_all(i + 1)

    def wait(r, carry):
        row_copy(i % 2, r, 0).wait()
        return carry

    lax.fori_loop(0, n_rows, wait, 0, unroll=8)


def _gather_norm_kernel(src_ref, tok_ref, h_hbm, g_ref, ssc_ref, ssl_ref, o_ref, buf, sem, *, tg, n_ctx):
    i = pl.program_id(0)
    _prefetch_rows(src_ref, h_hbm, buf, sem, i, pl.num_programs(0), tg)
    y = _rms(buf[i % 2], g_ref[...])
    is_ctx = tok_ref[...] < n_ctx
    scale = jnp.where(is_ctx, ssc_ref[1:2, :], ssl_ref[1:2, :])
    shift = jnp.where(is_ctx, ssc_ref[0:1, :], ssl_ref[0:1, :])
    o_ref[...] = (y * (1.0 + scale) + shift).astype(o_ref.dtype)


def _gather_norm(h, src, gain, ss, n_ctx):
    m, d = h.shape
    n_rows = src.shape[0]
    tg = ROW_BLOCK
    kern = functools.partial(_gather_norm_kernel, tg=tg, n_ctx=n_ctx)
    gs = pltpu.PrefetchScalarGridSpec(
        num_scalar_prefetch=1,
        grid=(n_rows // tg,),
        in_specs=[pl.BlockSpec((tg, 1), lambda i, *pf: (i, 0)),
                  pl.BlockSpec(memory_space=pl.ANY),
                  pl.BlockSpec((1, d), lambda i, *pf: (0, 0)),
                  pl.BlockSpec((None, 2, d), lambda i, *pf: (0, 0, 0)),
                  pl.BlockSpec((None, 2, d), lambda i, *pf: (1, 0, 0))],
        out_specs=pl.BlockSpec((tg, d), lambda i, *pf: (i, 0)),
        scratch_shapes=[pltpu.VMEM((2, tg, d), F32), pltpu.SemaphoreType.DMA((2,))])
    return pl.pallas_call(kern, grid_spec=gs, out_shape=jax.ShapeDtypeStruct((n_rows, d), BF16),
                          compiler_params=_cparams("arbitrary"),
                          name="moe_gather")(src, src.reshape(n_rows, 1), h, gain.reshape(1, d), ss, ss)


def _combine_kernel(pos_ref, h_ref, gate_ref, y_hbm, o_ref, buf, sem, *, tg, n_ctx):
    i = pl.program_id(0)
    _prefetch_rows(pos_ref, y_hbm, buf, sem, i, pl.num_programs(0), tg * TOP_K)
    row = i * tg + lax.broadcasted_iota(I32, (tg, 1), 0)
    g = jnp.where(row < n_ctx, gate_ref[0:1, :], gate_ref[1:2, :])
    slot = i % 2
    o_ref[...] = h_ref[...] + g * (buf[slot, 0:tg, :] + buf[slot, tg:TOP_K * tg, :])


def _combine(h, y, pos, gate, n_ctx):
    m, d = h.shape
    tg = ROW_BLOCK
    pos_tiles = pos.reshape(m // tg, tg, TOP_K).transpose(0, 2, 1).reshape(-1)
    kern = functools.partial(_combine_kernel, tg=tg, n_ctx=n_ctx)
    gs = pltpu.PrefetchScalarGridSpec(
        num_scalar_prefetch=1,
        grid=(m // tg,),
        in_specs=[pl.BlockSpec((tg, d), lambda i, *pf: (i, 0)),
                  pl.BlockSpec((2, d), lambda i, *pf: (0, 0)),
                  pl.BlockSpec(memory_space=pl.ANY)],
        out_specs=pl.BlockSpec((tg, d), lambda i, *pf: (i, 0)),
        scratch_shapes=[pltpu.VMEM((2, TOP_K * tg, d), F32), pltpu.SemaphoreType.DMA((2,))])
    return pl.pallas_call(kern, grid_spec=gs, out_shape=jax.ShapeDtypeStruct((m, d), F32),
                          compiler_params=_cparams("arbitrary"), name="moe_combine")(pos_tiles, h, gate, y)


def _moe(h, gain, ss, gate, w_router, w_gu, w_down, j, n_ctx):
    ids, wts = _router(h, gain, ss, w_router[j], n_ctx)
    src, row_w, pos, tile_expert, n_used = _moe_plan(ids[:, :TOP_K], wts[:, :TOP_K])
    xs = _gather_norm(h, src, gain, ss, n_ctx)
    group = (tile_expert + j * N_EXPERTS, n_used)
    act = _proj_gu(xs, w_gu.reshape((-1,) + w_gu.shape[2:]), None, group=group)
    y = _proj_moe_down(act, w_down.reshape((-1,) + w_down.shape[2:]), group, row_w)
    return _combine(h, y, pos, gate, n_ctx)


def kernel(x, c, ctx, c_ctx, ada_w, ada_b, norm_mix, norm_ffn, norm_final, conv_w_in, conv_w, conv_w_out,
           na_w_qkv, na_rpb, na_w_out, gqa_w_q, gqa_w_kv, gqa_q_norm, gqa_k_norm, gqa_w_out, ffn_w_gu,
           ffn_w_down, moe_w_router, moe_w_gu, moe_w_down):
    bsz, n_lat, d = x.shape
    n_ctx_full = ctx.shape[1]
    depth = ada_w.shape[0]
    assert bsz == 1 and n_ctx_full == ROW_BLOCK and n_lat % (GRID_W * NA_QROWS) == 0
    assert n_lat // GRID_W >= NA_WROWS

    cvec = jnp.zeros((8, d), F32).at[0].set(c_ctx).at[1].set(c[0])
    mod = _ada(cvec, ada_w, ada_b)[:, :2].reshape(depth, 2, 6, d)

    h = jnp.concatenate([ctx[0], x[0]], axis=0)
    n_ctx = n_ctx_full
    cos_t, sin_t = _rope_tables(n_ctx_full, n_lat)

    for i in range(depth):
        if i == depth - 1:
            h = h[n_ctx:]
            n_ctx = 0
        kind, j = i % N_MIXERS, i // N_MIXERS
        a = _norm_mod(h, norm_mix[i], mod[i, :, 0:2], n_ctx)
        gate1 = mod[i, :, 2]
        if kind == 0:
            b_gate, u = _proj_conv_in(a, conv_w_in, j)
            z = _conv_gate(b_gate, u, conv_w[j], n_ctx)
            h = _proj_resid(z, conv_w_out, j, h, gate1, n_ctx, "conv_out")
        elif kind == 1:
            qkv = _proj_plain(a, na_w_qkv, j, 0, 3 * d, "na_qkv")
            bias = _na_bias_tables(na_rpb[j], n_lat // GRID_W)
            o = _na_attention(qkv, bias, n_ctx)
            h = _proj_resid(o, na_w_out, j, h, gate1, n_ctx, "na_out")
        else:
            dq = GQA_HEADS * HEAD_DIM
            dkv = GQA_KV_HEADS * HEAD_DIM
            q = _proj_qk_rope(a, gqa_w_q, j, 0, dq, gqa_q_norm[j], cos_t, sin_t, HEAD_DIM ** -0.5 * LOG2_E, "gqa_q")
            k = _proj_qk_rope(a, gqa_w_kv, j, 0, dkv, gqa_k_norm[j], cos_t, sin_t, 1.0, "gqa_k")
            v = _proj_plain(a, gqa_w_kv, j, dkv, dkv, "gqa_v")
            o = _gqa_attention(q, k, v, n_ctx)
            h = _proj_resid(o, gqa_w_out, j, h, gate1, n_ctx, "gqa_out")
        gate2 = mod[i, :, 5]
        if i % 2 == 0:
            f = _norm_mod(h, norm_ffn[i], mod[i, :, 3:5], n_ctx)
            act = _proj_gu(f, ffn_w_gu, i // 2)
            h = _proj_resid(act, ffn_w_down, i // 2, h, gate2, n_ctx, "ffn_down")
        else:
            h = _moe(h, norm_ffn[i], mod[i, :, 3:5], gate2, moe_w_router, moe_w_gu, moe_w_down, i // 2, n_ctx)
    return _final_norm(h, norm_final)[None]
```

```python
import functools

import numpy as np
import jax
import jax.numpy as jnp
from jax import lax
from jax.experimental import pallas as pl
from jax.experimental.pallas import tpu as pltpu

F32 = jnp.float32
BF16 = jnp.bfloat16
I32 = jnp.int32

EPS = 1e-6
GRID_W = 64
N_MIXERS = 3
NA_HEADS = 16
NA_KH = 8
NA_KW = 16
NA_QROWS = 4
NA_WROWS = NA_QROWS + NA_KH
GQA_HEADS = 16
GQA_KV_HEADS = 4
HEAD_DIM = 128
ROPE_THETA = 10000.0
N_EXPERTS = 8
TOP_K = 2
LANES = 128
ROW_BLOCK = 256
MASKED = -1e30
LOG2_E = float(np.log2(np.e))
VMEM_LIMIT = 60 * 1024 * 1024


def _cparams(*sem):
    return pltpu.CompilerParams(dimension_semantics=sem, vmem_limit_bytes=VMEM_LIMIT)


def _pick(n, cands):
    for c in cands:
        if n % c == 0:
            return c
    raise ValueError(f"no tile for {n} in {cands}")


def _dot(a, b):
    return jnp.dot(a, b, preferred_element_type=F32)


def _dot_nt(a, b):
    return lax.dot_general(a, b, (((1,), (1,)), ((), ())), preferred_element_type=F32)


def _silu(x):
    return x * (1.0 / (1.0 + jnp.exp(-x)))


def _rms(x, gain):
    return x * lax.rsqrt(jnp.mean(x * x, axis=-1, keepdims=True) + EPS) * gain


def _ada_kernel(c_ref, w_ref, b_ref, o_ref):
    s = _silu(c_ref[...])
    o_ref[...] = jnp.dot(s, w_ref[...], precision=lax.Precision.HIGHEST,
                         preferred_element_type=F32) + b_ref[...]


def _ada(cvec, ada_w, ada_b):
    depth, d, n = ada_w.shape
    tn = _pick(n, (1024, 512, 256, 128))
    return pl.pallas_call(
        _ada_kernel,
        grid=(depth, n // tn),
        in_specs=[pl.BlockSpec((8, d), lambda i, j: (0, 0)),
                  pl.BlockSpec((None, d, tn), lambda i, j: (i, 0, j)),
                  pl.BlockSpec((None, 1, tn), lambda i, j: (i, 0, j))],
        out_specs=pl.BlockSpec((None, 8, tn), lambda i, j: (i, 0, j)),
        out_shape=jax.ShapeDtypeStruct((depth, 8, n), F32),
        compiler_params=_cparams("arbitrary", "arbitrary"),
        name="ada",
    )(cvec, ada_w, ada_b.reshape(depth, 1, n))


def _norm_mod_kernel(h_ref, g_ref, ss_ref, a_ref):
    y = _rms(h_ref[...], g_ref[...])
    a_ref[...] = (y * (1.0 + ss_ref[1:2, :]) + ss_ref[0:1, :]).astype(a_ref.dtype)


def _seg_map(n_ctx_blocks):
    return lambda i: (jnp.where(i < n_ctx_blocks, 0, 1), 0, 0)


def _norm_mod(h, gain, ss, n_ctx):
    m, d = h.shape
    tm = ROW_BLOCK
    return pl.pallas_call(
        _norm_mod_kernel,
        grid=(m // tm,),
        in_specs=[pl.BlockSpec((tm, d), lambda i: (i, 0)),
                  pl.BlockSpec((1, d), lambda i: (0, 0)),
                  pl.BlockSpec((None, 2, d), _seg_map(n_ctx // tm))],
        out_specs=pl.BlockSpec((tm, d), lambda i: (i, 0)),
        out_shape=jax.ShapeDtypeStruct((m, d), BF16),
        compiler_params=_cparams("arbitrary"),
        name="norm_mod",
    )(h, gain.reshape(1, d), ss)


def _router_kernel(h_ref, g_ref, ss_ref, wr_ref, ids_ref, wts_ref):
    y = _rms(h_ref[...], g_ref[...])
    f = y * (1.0 + ss_ref[1:2, :]) + ss_ref[0:1, :]
    logits = jnp.dot(f, wr_ref[...], precision=lax.Precision.HIGHEST, preferred_element_type=F32)
    lane = lax.broadcasted_iota(I32, logits.shape, 1)
    neg = jnp.float32(-jnp.inf)
    l1 = jnp.where(lane < N_EXPERTS, logits, neg)
    m1 = jnp.max(l1, axis=-1, keepdims=True)
    i1 = jnp.min(jnp.where(l1 == m1, lane, LANES), axis=-1, keepdims=True)
    l2 = jnp.where(lane == i1, neg, l1)
    m2 = jnp.max(l2, axis=-1, keepdims=True)
    i2 = jnp.min(jnp.where(l2 == m2, lane, LANES), axis=-1, keepdims=True)
    e2 = jnp.exp(m2 - m1)
    w1 = 1.0 / (1.0 + e2)
    w2 = e2 / (1.0 + e2)
    ids_ref[...] = jnp.where(lane == 0, i1, jnp.where(lane == 1, i2, 0))
    wts_ref[...] = jnp.where(lane == 0, w1, jnp.where(lane == 1, w2, 0.0))


def _router(h, gain, ss, w_router, n_ctx):
    m, d = h.shape
    tm = ROW_BLOCK
    wr = jnp.zeros((d, LANES), F32).at[:, :N_EXPERTS].set(w_router)
    return pl.pallas_call(
        _router_kernel,
        grid=(m // tm,),
        in_specs=[pl.BlockSpec((tm, d), lambda i: (i, 0)),
                  pl.BlockSpec((1, d), lambda i: (0, 0)),
                  pl.BlockSpec((None, 2, d), _seg_map(n_ctx // tm)),
                  pl.BlockSpec((d, LANES), lambda i: (0, 0))],
        out_specs=[pl.BlockSpec((tm, LANES), lambda i: (i, 0)),
                   pl.BlockSpec((tm, LANES), lambda i: (i, 0))],
        out_shape=[jax.ShapeDtypeStruct((m, LANES), I32),
                   jax.ShapeDtypeStruct((m, LANES), F32)],
        compiler_params=_cparams("arbitrary"),
        name="router",
    )(h, gain.reshape(1, d), ss, wr)


def _final_norm_kernel(h_ref, g_ref, o_ref):
    o_ref[...] = _rms(h_ref[...], g_ref[...])


def _final_norm(h, gain):
    m, d = h.shape
    tm = ROW_BLOCK
    return pl.pallas_call(
        _final_norm_kernel,
        grid=(m // tm,),
        in_specs=[pl.BlockSpec((tm, d), lambda i: (i, 0)),
                  pl.BlockSpec((1, d), lambda i: (0, 0))],
        out_specs=pl.BlockSpec((tm, d), lambda i: (i, 0)),
        out_shape=jax.ShapeDtypeStruct((m, d), F32),
        compiler_params=_cparams("arbitrary"),
        name="final_norm",
    )(h, gain.reshape(1, d))


def _mm(x, w, *, sel, col_offs, n_tiles, tn, tm, epi, extra=(), extra_specs=(), out_shape, out_specs, name):
    m_rows, k = x.shape
    n_w = len(col_offs)
    n_ex = len(extra)
    n_out = len(out_shape)

    def kern(*refs):
        x_ref = refs[0]
        w_refs = refs[1:1 + n_w]
        ex = refs[1 + n_w:1 + n_w + n_ex]
        outs = refs[1 + n_w + n_ex:1 + n_w + n_ex + n_out]
        wb = refs[-1]
        m = pl.program_id(1)

        @pl.when(m == 0)
        def _():
            for j in range(n_w):
                wb[j] = w_refs[j][...].astype(BF16)

        xv = x_ref[...]
        accs = [_dot(xv, wb[j]) for j in range(n_w)]
        for o, r in zip(outs, epi(accs, m, *ex)):
            o[...] = r.astype(o.dtype)

    def w_map(off):
        return lambda n, m: (sel, 0, n + off)

    in_specs = ([pl.BlockSpec((tm, k), lambda n, m: (m, 0))]
                + [pl.BlockSpec((None, k, tn), w_map(off)) for off in col_offs]
                + list(extra_specs))
    return pl.pallas_call(kern, grid=(n_tiles, m_rows // tm), in_specs=in_specs, out_specs=out_specs,
                          out_shape=out_shape, scratch_shapes=[pltpu.VMEM((n_w, k, tn), BF16)],
                          compiler_params=_cparams("arbitrary", "arbitrary"), name=name)(x, *([w] * n_w), *extra)


def _mm_grouped(x, w, starts, counts, slab0, *, col_offs, n_tiles, tn, epi, out_dtype, n_out_cols, name):
    n_rows, k = x.shape
    n_w = len(col_offs)
    ts = MOE_TILE

    def kern(st_ref, ct_ref, s0_ref, x_hbm, *rest):
        w_refs = rest[:n_w]
        o_hbm = rest[n_w]
        xbuf, obuf, wb, pend, sin, sout = rest[n_w + 1:]
        n, e = pl.program_id(0), pl.program_id(1)
        base, cnt = st_ref[e], ct_ref[e]

        def x_copy(r, slot):
            row0 = pl.multiple_of(base + r * ts, ts)
            return pltpu.make_async_copy(x_hbm.at[pl.ds(row0, ts), :], xbuf.at[slot], sin.at[slot])

        def o_copy(r, slot):
            row0 = pl.multiple_of(base + r * ts, ts)
            col0 = pl.multiple_of(n * tn, tn)
            return pltpu.make_async_copy(obuf.at[slot], o_hbm.at[pl.ds(row0, ts), pl.ds(col0, tn)], sout.at[slot])

        def o_drain(slot):
            @pl.when(pend[slot] == 1)
            def _():
                o_copy(0, slot).wait()
                pend[slot] = 0

        def o_emit(r, slot, tile):
            o_drain(slot)
            obuf[slot] = tile
            o_copy(r, slot).start()
            pend[slot] = 1

        @pl.when(jnp.logical_and(n == 0, e == 0))
        def _():
            pend[0] = 0
            pend[1] = 0

        @pl.when(cnt > 0)
        def _():
            x_copy(0, 0).start()

        for j in range(n_w):
            wb[j] = w_refs[j][...].astype(BF16)

        def body(r, carry):
            slot = r % 2

            @pl.when(r + 1 < cnt)
            def _():
                x_copy(r + 1, 1 - slot).start()

            x_copy(r, slot).wait()
            xv = xbuf[slot]
            o_emit(r, slot, epi([_dot(xv, wb[j]) for j in range(n_w)]).astype(out_dtype))
            return carry

        lax.fori_loop(0, cnt, body, 0)

        @pl.when(e == pl.num_programs(1) - 1)
        def _():
            def zero_tile(r, carry):
                o_emit(r, r % 2, jnp.zeros((ts, tn), out_dtype))
                return carry

            lax.fori_loop(cnt, (n_rows - base) // ts, zero_tile, 0)

        @pl.when(jnp.logical_and(n == pl.num_programs(0) - 1, e == pl.num_programs(1) - 1))
        def _():
            o_drain(0)
            o_drain(1)

    def w_map(off):
        return lambda n, e, st, ct, s0: (s0[0] + e, 0, n + off)

    gs = pltpu.PrefetchScalarGridSpec(
        num_scalar_prefetch=3,
        grid=(n_tiles, N_EXPERTS),
        in_specs=[pl.BlockSpec(memory_space=pl.ANY)] + [pl.BlockSpec((None, k, tn), w_map(off)) for off in col_offs],
        out_specs=pl.BlockSpec(memory_space=pl.ANY),
        scratch_shapes=[pltpu.VMEM((2, ts, k), BF16), pltpu.VMEM((2, ts, tn), out_dtype),
                        pltpu.VMEM((n_w, k, tn), BF16), pltpu.SMEM((2,), I32),
                        pltpu.SemaphoreType.DMA((2,)), pltpu.SemaphoreType.DMA((2,))])
    return pl.pallas_call(kern, grid_spec=gs, out_shape=jax.ShapeDtypeStruct((n_rows, n_out_cols), out_dtype),
                          compiler_params=_cparams("arbitrary", "arbitrary"),
                          name=name)(starts, counts, slab0, x, *([w] * n_w))


def _row_tile(m_rows):
    return _pick(m_rows, (768, 512, 256))


def _proj_plain(x, w, sel, col0, n_cols, name):
    m_rows = x.shape[0]
    tn = _pick(n_cols, (1024, 512, 256))
    assert col0 % tn == 0
    tm = _row_tile(m_rows)
    return _mm(x, w, sel=sel, col_offs=(col0 // tn,), n_tiles=n_cols // tn, tn=tn, tm=tm,
               epi=lambda accs, m: (accs[0],),
               out_shape=[jax.ShapeDtypeStruct((m_rows, n_cols), BF16)],
               out_specs=[pl.BlockSpec((tm, tn), lambda n, m: (m, n))], name=name)[0]


def _proj_conv_in(x, w, sel):
    m_rows = x.shape[0]
    d3 = w.shape[2] // 3
    tn = _pick(d3, (512, 256))
    tm = _row_tile(m_rows)
    nt = d3 // tn
    o = jax.ShapeDtypeStruct((m_rows, d3), BF16)
    spec = pl.BlockSpec((tm, tn), lambda n, m: (m, n))
    return _mm(x, w, sel=sel, col_offs=(0, nt, 2 * nt), n_tiles=nt, tn=tn, tm=tm,
               epi=lambda accs, m: (accs[0], accs[1] * accs[2]),
               out_shape=[o, o], out_specs=[spec, spec], name="conv_in")


def _proj_gu(x, w, sel):
    m_rows = x.shape[0]
    f = w.shape[2] // 2
    tn = _pick(f, (512, 256))
    tm = _row_tile(m_rows)
    nt = f // tn
    return _mm(x, w, sel=sel, col_offs=(0, nt), n_tiles=nt, tn=tn, tm=tm,
               epi=lambda accs, m: (_silu(accs[0]) * accs[1],),
               out_shape=[jax.ShapeDtypeStruct((m_rows, f), BF16)],
               out_specs=[pl.BlockSpec((tm, tn), lambda n, m: (m, n))], name="ffn_gu")[0]


def _moe_gu(x, w, starts, counts, slab0):
    f = w.shape[2] // 2
    tn = _pick(f, (512, 256))
    nt = f // tn
    return _mm_grouped(x, w, starts, counts, slab0, col_offs=(0, nt), n_tiles=nt, tn=tn,
                       epi=lambda accs: _silu(accs[0]) * accs[1], out_dtype=BF16, n_out_cols=f, name="moe_gu")


def _moe_down(x, w, starts, counts, slab0):
    n_cols = w.shape[2]
    tn = 512
    return _mm_grouped(x, w, starts, counts, slab0, col_offs=(0,), n_tiles=n_cols // tn, tn=tn,
                       epi=lambda accs: accs[0], out_dtype=F32, n_out_cols=n_cols, name="moe_down")


def _proj_qk_rope(x, w, sel, col0, n_cols, gain, cos_t, sin_t, scale, name):
    m_rows = x.shape[0]
    tn = _pick(n_cols, (512, 256, 128))
    assert col0 % tn == 0
    tm = _row_tile(m_rows)

    def epi(accs, m, gain_ref, cos_ref, sin_ref):
        lane = lax.broadcasted_iota(I32, (1, HEAD_DIM), 1)
        first = (lane & (HEAD_DIM // 4)) == 0
        cos_v, sin_v, g = cos_ref[...], sin_ref[...], gain_ref[...]
        heads = []
        for hh in range(tn // HEAD_DIM):
            y = _rms(accs[0][:, hh * HEAD_DIM:(hh + 1) * HEAD_DIM], g)
            rot = jnp.where(first, pltpu.roll(y, HEAD_DIM - HEAD_DIM // 4, 1), pltpu.roll(y, HEAD_DIM // 4, 1))
            heads.append((y * cos_v + rot * sin_v) * scale)
        return (jnp.concatenate(heads, axis=1),)

    return _mm(x, w, sel=sel, col_offs=(col0 // tn,), n_tiles=n_cols // tn, tn=tn, tm=tm, epi=epi,
               extra=(gain.reshape(1, HEAD_DIM), cos_t, sin_t),
               extra_specs=(pl.BlockSpec((1, HEAD_DIM), lambda n, m: (0, 0)),
                            pl.BlockSpec((tm, HEAD_DIM), lambda n, m: (m, 0)),
                            pl.BlockSpec((tm, HEAD_DIM), lambda n, m: (m, 0))),
               out_shape=[jax.ShapeDtypeStruct((m_rows, n_cols), BF16)],
               out_specs=[pl.BlockSpec((tm, tn), lambda n, m: (m, n))], name=name)[0]


def _proj_resid(x, w, sel, h, gate, n_ctx, name):
    m_rows, k = x.shape
    n_cols = w.shape[2]
    big_k = k > 4096
    tn = 512 if big_k else _pick(n_cols, (1024, 512, 256))
    tm = _pick(m_rows, (512, 384, 256)) if big_k else _row_tile(m_rows)

    def epi(accs, m, h_ref, gate_ref):
        row = m * tm + lax.broadcasted_iota(I32, (tm, 1), 0)
        g = jnp.where(row < n_ctx, gate_ref[0:1, :], gate_ref[1:2, :])
        return (h_ref[...] + g * accs[0],)

    return _mm(x, w, sel=sel, col_offs=(0,), n_tiles=n_cols // tn, tn=tn, tm=tm, epi=epi,
               extra=(h, gate),
               extra_specs=(pl.BlockSpec((tm, tn), lambda n, m: (m, n)),
                            pl.BlockSpec((2, tn), lambda n, m: (0, n))),
               out_shape=[jax.ShapeDtypeStruct((m_rows, n_cols), F32)],
               out_specs=[pl.BlockSpec((tm, tn), lambda n, m: (m, n))], name=name)[0]


def _conv_gate_kernel(b_ref, u_ref, up_ref, un_ref, cw_ref, z_ref, *, tm, n_ctx_blocks, n_blocks):
    i = pl.program_id(0)
    u = u_ref[...].astype(F32)
    prev_ok = jnp.logical_and(i != 0, i != n_ctx_blocks).astype(F32)
    next_ok = jnp.logical_and(i != n_ctx_blocks - 1, i != n_blocks - 1).astype(F32)
    prow = up_ref[15:16, :].astype(F32) * prev_ok
    nrow = un_ref[0:1, :].astype(F32) * next_ok
    row = lax.broadcasted_iota(I32, (tm, 1), 0)
    um1 = jnp.where(row == 0, prow, pltpu.roll(u, 1, 0))
    up1 = jnp.where(row == tm - 1, nrow, pltpu.roll(u, tm - 1, 0))
    conv = cw_ref[0:1, :] * um1 + cw_ref[1:2, :] * u + cw_ref[2:3, :] * up1
    z_ref[...] = (b_ref[...].astype(F32) * conv).astype(z_ref.dtype)


def _conv_gate(b, u, conv_w, n_ctx):
    m, d = u.shape
    tm = ROW_BLOCK
    tc = _pick(d, (1024, 512, 256, 128))
    hb = 16
    nb = m // tm
    last_hb = m // hb - 1
    kern = functools.partial(_conv_gate_kernel, tm=tm, n_ctx_blocks=n_ctx // tm, n_blocks=nb)
    return pl.pallas_call(
        kern,
        grid=(nb, d // tc),
        in_specs=[pl.BlockSpec((tm, tc), lambda i, j: (i, j)),
                  pl.BlockSpec((tm, tc), lambda i, j: (i, j)),
                  pl.BlockSpec((hb, tc), lambda i, j: (jnp.maximum(i * (tm // hb) - 1, 0), j)),
                  pl.BlockSpec((hb, tc), lambda i, j: (jnp.minimum((i + 1) * (tm // hb), last_hb), j)),
                  pl.BlockSpec((3, tc), lambda i, j: (0, j))],
        out_specs=pl.BlockSpec((tm, tc), lambda i, j: (i, j)),
        out_shape=jax.ShapeDtypeStruct((m, d), BF16),
        compiler_params=_cparams("arbitrary", "arbitrary"),
        name="conv_gate",
    )(b, u, u, u, conv_w)


def _na_bias_tables(rpb, rows):
    nb = rows // NA_QROWS
    n_h = rpb.shape[0]
    c = np.arange(GRID_W)[:, None]
    kc = np.arange(GRID_W)[None, :]
    c0 = np.clip(c - NA_KW // 2, 0, GRID_W - NA_KW)
    c_valid = (kc >= c0) & (kc < c0 + NA_KW)
    c_sel = (kc - c + NA_KW - 1)[:, :, None] == np.arange(2 * NA_KW - 1)[None, None, :]
    c_sel = (c_sel & c_valid[:, :, None]).astype(np.float32)
    r_sel, r_valid = [], []
    for b in (0, 1, nb - 1):
        w0 = NA_QROWS * int(np.clip(b - 1, 0, nb - 3))
        ar = NA_QROWS * b + np.arange(NA_QROWS)[:, None]
        kr = w0 + np.arange(NA_WROWS)[None, :]
        r0 = np.clip(ar - NA_KH // 2, 0, rows - NA_KH)
        ok = (kr >= r0) & (kr < r0 + NA_KH)
        sel = (kr - ar + NA_KH - 1)[:, :, None] == np.arange(2 * NA_KH - 1)[None, None, :]
        r_sel.append((sel & ok[:, :, None]).astype(np.float32))
        r_valid.append(ok)
    r_sel, r_valid = np.stack(r_sel), np.stack(r_valid)
    t = jnp.einsum("pijr,hrd,ckd->phicjk", jnp.asarray(r_sel), rpb, jnp.asarray(c_sel),
                   precision=lax.Precision.HIGHEST)
    valid = r_valid[:, None, :, None, :, None] & c_valid[None, None, None, :, None, :]
    t = jnp.where(valid, t, MASKED).reshape(3, n_h, NA_QROWS * GRID_W, NA_WROWS * GRID_W)
    return jnp.concatenate([jnp.full((1,) + t.shape[1:], MASKED, F32), t], axis=0)


def _na_kernel(q_ref, k0_ref, k1_ref, k2_ref, kc_ref, v0_ref, v1_ref, v2_ref, vc_ref, bias_ref, o_ref, *, scale):
    blk = ROW_BLOCK
    for h in range(NA_HEADS):
        hs = slice(h * HEAD_DIM, (h + 1) * HEAD_DIM)
        q = q_ref[:, hs]
        ss = [_dot_nt(q, kr[:, hs]) * scale + bias_ref[h, :, j * blk:(j + 1) * blk]
              for j, kr in enumerate((k0_ref, k1_ref, k2_ref))]
        ss.append(_dot_nt(q, kc_ref[:, hs]) * scale)
        mx = functools.reduce(jnp.maximum, [jnp.max(s, axis=-1, keepdims=True) for s in ss])
        ps = [jnp.exp(s - mx) for s in ss]
        den = functools.reduce(jnp.add, [jnp.sum(p, axis=-1, keepdims=True) for p in ps])
        acc = functools.reduce(jnp.add, [_dot(p.astype(BF16), vr[:, hs])
                                         for p, vr in zip(ps, (v0_ref, v1_ref, v2_ref, vc_ref))])
        o_ref[:, hs] = (acc / den).astype(o_ref.dtype)


def _na_attention(qkv, bias, n_ctx):
    m, d3 = qkv.shape
    d = d3 // 3
    blk = ROW_BLOCK
    assert n_ctx == blk and NA_QROWS * GRID_W == blk
    nq = m // blk
    nb = nq - 1

    def kv_map(j, col):
        return lambda g: (1 + jnp.clip(g - 2, 0, nb - 3) + j, col)

    def bias_map(g):
        return (jnp.where(g == 0, 0, jnp.where(g == 1, 1, jnp.where(g == nq - 1, 3, 2))), 0, 0, 0)

    blkspec = lambda imap: pl.BlockSpec((blk, d), imap)
    return pl.pallas_call(
        functools.partial(_na_kernel, scale=HEAD_DIM ** -0.5),
        grid=(nq,),
        in_specs=[blkspec(lambda g: (g, 0)),
                  blkspec(kv_map(0, 1)), blkspec(kv_map(1, 1)), blkspec(kv_map(2, 1)), blkspec(lambda g: (0, 1)),
                  blkspec(kv_map(0, 2)), blkspec(kv_map(1, 2)), blkspec(kv_map(2, 2)), blkspec(lambda g: (0, 2)),
                  pl.BlockSpec((None, NA_HEADS, blk, 3 * blk), bias_map, pipeline_mode=pl.Buffered(1))],
        out_specs=blkspec(lambda g: (g, 0)),
        out_shape=jax.ShapeDtypeStruct((m, d), BF16),
        compiler_params=_cparams("arbitrary"),
        name="na_attention",
    )(qkv, qkv, qkv, qkv, qkv, qkv, qkv, qkv, qkv, bias)


def _rope_tables(n_ctx, n_lat):
    t = jnp.arange(n_lat)
    row = (t // GRID_W).astype(F32)
    col = (t % GRID_W).astype(F32)
    quarter = HEAD_DIM // 4
    inv_freq = ROPE_THETA ** (-jnp.arange(quarter, dtype=F32) / quarter)
    ar, ac = row[:, None] * inv_freq, col[:, None] * inv_freq
    cos_t = jnp.concatenate([jnp.cos(ar), jnp.cos(ar), jnp.cos(ac), jnp.cos(ac)], axis=1)
    sin_t = jnp.concatenate([-jnp.sin(ar), jnp.sin(ar), -jnp.sin(ac), jnp.sin(ac)], axis=1)
    cos_t = jnp.concatenate([jnp.ones((n_ctx, HEAD_DIM), F32), cos_t], axis=0)
    sin_t = jnp.concatenate([jnp.zeros((n_ctx, HEAD_DIM), F32), sin_t], axis=0)
    return cos_t, sin_t


def _gqa_kernel(q_ref, k_ref, v_ref, o_ref, *, n_ctx, tk, n_lat_chunks, group):
    is_ctx = pl.program_id(1) * ROW_BLOCK < n_ctx
    n_it = jnp.where(is_ctx, 0, n_lat_chunks)
    qs = [q_ref[:, g * HEAD_DIM:(g + 1) * HEAD_DIM] for g in range(group)]
    kc, vc = k_ref[0:n_ctx, :], v_ref[0:n_ctx, :]
    state = []
    for g in range(group):
        s = _dot_nt(qs[g], kc)
        mx = jnp.max(s, axis=-1, keepdims=True)
        state += [mx, _dot(jnp.exp2(s - mx).astype(BF16), vc)]

    def body(c, carry):
        off = pl.multiple_of(n_ctx + c * tk, ROW_BLOCK)
        kk, vv = k_ref[pl.ds(off, tk), :], v_ref[pl.ds(off, tk), :]
        out = []
        for g in range(group):
            mx, acc = carry[2 * g:2 * g + 2]
            s = _dot_nt(qs[g], kk)
            mx_new = jnp.maximum(mx, jnp.max(s, axis=-1, keepdims=True))
            out += [mx_new, jnp.exp2(mx - mx_new) * acc + _dot(jnp.exp2(s - mx_new).astype(BF16), vv)]
        return tuple(out)

    state = lax.fori_loop(0, n_it, body, tuple(state))
    for g in range(group):
        acc = state[2 * g + 1]
        o_ref[:, g * HEAD_DIM:(g + 1) * HEAD_DIM] = (acc[:, :HEAD_DIM] / acc[:, HEAD_DIM:]).astype(o_ref.dtype)


def _gqa_attention(q, k, v, n_ctx):
    m, dq = q.shape
    group = GQA_HEADS // GQA_KV_HEADS
    gw = group * HEAD_DIM
    tq = ROW_BLOCK
    n_lat = m - n_ctx
    tk = _pick(n_lat, (2048, 1024, 512, 256))
    kern = functools.partial(_gqa_kernel, n_ctx=n_ctx, tk=tk, n_lat_chunks=n_lat // tk, group=group)
    v = jnp.concatenate([v.reshape(m, GQA_KV_HEADS, HEAD_DIM), jnp.ones((m, GQA_KV_HEADS, HEAD_DIM), v.dtype)],
                        axis=2).reshape(m, 2 * GQA_KV_HEADS * HEAD_DIM)
    return pl.pallas_call(
        kern,
        grid=(GQA_KV_HEADS, m // tq),
        in_specs=[pl.BlockSpec((tq, gw), lambda kh, i: (i, kh)),
                  pl.BlockSpec((m, HEAD_DIM), lambda kh, i: (0, kh)),
                  pl.BlockSpec((m, 2 * HEAD_DIM), lambda kh, i: (0, kh))],
        out_specs=pl.BlockSpec((tq, gw), lambda kh, i: (i, kh)),
        out_shape=jax.ShapeDtypeStruct((m, dq), BF16),
        compiler_params=_cparams("arbitrary", "arbitrary"),
        name="gqa_attention",
    )(q, k, v)


MOE_TILE = 256


def _moe_plan(ids):
    m = ids.shape[0]
    tm = MOE_TILE
    n_rows = -(-(TOP_K * m + N_EXPERTS * (tm - 1)) // tm) * tm
    flat_e = ids.reshape(-1)
    onehot = (flat_e[:, None] == jnp.arange(N_EXPERTS, dtype=I32)[None, :]).astype(I32)
    csum = jnp.cumsum(onehot, axis=0)
    rank = jnp.sum(csum * onehot, axis=1) - 1
    tiles = (csum[-1] + tm - 1) // tm
    starts = (jnp.cumsum(tiles) - tiles) * tm
    pos = jnp.sum(starts[None, :] * onehot, axis=1) + rank
    tok = jnp.arange(TOP_K * m, dtype=I32) // TOP_K
    src = jnp.zeros((n_rows,), I32).at[pos].set(tok)
    return src, pos.astype(I32), starts.astype(I32), tiles.astype(I32)


def _prefetch_rows(idx_ref, src_hbm, buf, sem, i, n_steps, n_rows):
    def row_copy(slot, r, row):
        return pltpu.make_async_copy(src_hbm.at[pl.ds(row, 1), :], buf.at[slot, pl.ds(r, 1), :], sem.at[slot])

    def start_all(step):
        slot = step % 2

        def start(r, carry):
            row_copy(slot, r, idx_ref[step * n_rows + r]).start()
            return carry

        lax.fori_loop(0, n_rows, start, 0, unroll=8)

    @pl.when(i == 0)
    def _():
        start_all(i)

    @pl.when(i + 1 < n_steps)
    def _():
        start_all(i + 1)

    def wait(r, carry):
        row_copy(i % 2, r, 0).wait()
        return carry

    lax.fori_loop(0, n_rows, wait, 0, unroll=8)


def _gather_norm_kernel(src_ref, tok_ref, h_hbm, g_ref, ssc_ref, ssl_ref, o_ref, buf, sem, *, tg, n_ctx):
    i = pl.program_id(0)
    _prefetch_rows(src_ref, h_hbm, buf, sem, i, pl.num_programs(0), tg)
    y = _rms(buf[i % 2], g_ref[...])
    is_ctx = tok_ref[...] < n_ctx
    scale = jnp.where(is_ctx, ssc_ref[1:2, :], ssl_ref[1:2, :])
    shift = jnp.where(is_ctx, ssc_ref[0:1, :], ssl_ref[0:1, :])
    o_ref[...] = (y * (1.0 + scale) + shift).astype(o_ref.dtype)


def _gather_norm(h, src, gain, ss, n_ctx):
    m, d = h.shape
    n_rows = src.shape[0]
    tg = ROW_BLOCK
    kern = functools.partial(_gather_norm_kernel, tg=tg, n_ctx=n_ctx)
    gs = pltpu.PrefetchScalarGridSpec(
        num_scalar_prefetch=1,
        grid=(n_rows // tg,),
        in_specs=[pl.BlockSpec((tg, 1), lambda i, *pf: (i, 0)),
                  pl.BlockSpec(memory_space=pl.ANY),
                  pl.BlockSpec((1, d), lambda i, *pf: (0, 0)),
                  pl.BlockSpec((None, 2, d), lambda i, *pf: (0, 0, 0)),
                  pl.BlockSpec((None, 2, d), lambda i, *pf: (1, 0, 0))],
        out_specs=pl.BlockSpec((tg, d), lambda i, *pf: (i, 0)),
        scratch_shapes=[pltpu.VMEM((2, tg, d), F32), pltpu.SemaphoreType.DMA((2,))])
    return pl.pallas_call(kern, grid_spec=gs, out_shape=jax.ShapeDtypeStruct((n_rows, d), BF16),
                          compiler_params=_cparams("arbitrary"),
                          name="moe_gather")(src, src.reshape(n_rows, 1), h, gain.reshape(1, d), ss, ss)


def _combine_kernel(pos_ref, h_ref, gate_ref, wts_ref, y_hbm, o_ref, buf, sem, *, tg, n_ctx):
    i = pl.program_id(0)
    _prefetch_rows(pos_ref, y_hbm, buf, sem, i, pl.num_programs(0), tg * TOP_K)
    row = i * tg + lax.broadcasted_iota(I32, (tg, 1), 0)
    g = jnp.where(row < n_ctx, gate_ref[0:1, :], gate_ref[1:2, :])
    slot = i % 2
    mix = wts_ref[:, 0:1] * buf[slot, 0:tg, :] + wts_ref[:, 1:2] * buf[slot, tg:TOP_K * tg, :]
    o_ref[...] = h_ref[...] + g * mix


def _combine(h, y, pos, wts, gate, n_ctx):
    m, d = h.shape
    tg = ROW_BLOCK
    pos_tiles = pos.reshape(m // tg, tg, TOP_K).transpose(0, 2, 1).reshape(-1)
    kern = functools.partial(_combine_kernel, tg=tg, n_ctx=n_ctx)
    gs = pltpu.PrefetchScalarGridSpec(
        num_scalar_prefetch=1,
        grid=(m // tg,),
        in_specs=[pl.BlockSpec((tg, d), lambda i, *pf: (i, 0)),
                  pl.BlockSpec((2, d), lambda i, *pf: (0, 0)),
                  pl.BlockSpec((tg, LANES), lambda i, *pf: (i, 0)),
                  pl.BlockSpec(memory_space=pl.ANY)],
        out_specs=pl.BlockSpec((tg, d), lambda i, *pf: (i, 0)),
        scratch_shapes=[pltpu.VMEM((2, TOP_K * tg, d), F32), pltpu.SemaphoreType.DMA((2,))])
    return pl.pallas_call(kern, grid_spec=gs, out_shape=jax.ShapeDtypeStruct((m, d), F32),
                          compiler_params=_cparams("arbitrary"), name="moe_combine")(pos_tiles, h, gate, wts, y)


def _moe(h, gain, ss, gate, w_router, w_gu, w_down, j, n_ctx):
    ids, wts = _router(h, gain, ss, w_router[j], n_ctx)
    src, pos, starts, tiles = _moe_plan(ids[:, :TOP_K])
    xs = _gather_norm(h, src, gain, ss, n_ctx)
    slab0 = jnp.full((1,), j * N_EXPERTS, I32)
    act = _moe_gu(xs, w_gu.reshape((-1,) + w_gu.shape[2:]), starts, tiles, slab0)
    y = _moe_down(act, w_down.reshape((-1,) + w_down.shape[2:]), starts, tiles, slab0)
    return _combine(h, y, pos, wts, gate, n_ctx)


def kernel(x, c, ctx, c_ctx, ada_w, ada_b, norm_mix, norm_ffn, norm_final, conv_w_in, conv_w, conv_w_out,
           na_w_qkv, na_rpb, na_w_out, gqa_w_q, gqa_w_kv, gqa_q_norm, gqa_k_norm, gqa_w_out, ffn_w_gu,
           ffn_w_down, moe_w_router, moe_w_gu, moe_w_down):
    bsz, n_lat, d = x.shape
    n_ctx_full = ctx.shape[1]
    depth = ada_w.shape[0]
    assert bsz == 1 and n_ctx_full == ROW_BLOCK and n_lat % (GRID_W * NA_QROWS) == 0
    assert n_lat // GRID_W >= NA_WROWS

    cvec = jnp.zeros((8, d), F32).at[0].set(c_ctx).at[1].set(c[0])
    mod = _ada(cvec, ada_w, ada_b)[:, :2].reshape(depth, 2, 6, d)

    h = jnp.concatenate([ctx[0], x[0]], axis=0)
    n_ctx = n_ctx_full
    cos_t, sin_t = _rope_tables(n_ctx_full, n_lat)

    for i in range(depth):
        if i == depth - 1:
            h = h[n_ctx:]
            n_ctx = 0
        kind, j = i % N_MIXERS, i // N_MIXERS
        a = _norm_mod(h, norm_mix[i], mod[i, :, 0:2], n_ctx)
        gate1 = mod[i, :, 2]
        if kind == 0:
            b_gate, u = _proj_conv_in(a, conv_w_in, j)
            z = _conv_gate(b_gate, u, conv_w[j], n_ctx)
            h = _proj_resid(z, conv_w_out, j, h, gate1, n_ctx, "conv_out")
        elif kind == 1:
            qkv = _proj_plain(a, na_w_qkv, j, 0, 3 * d, "na_qkv")
            bias = _na_bias_tables(na_rpb[j], n_lat // GRID_W)
            o = _na_attention(qkv, bias, n_ctx)
            h = _proj_resid(o, na_w_out, j, h, gate1, n_ctx, "na_out")
        else:
            dq = GQA_HEADS * HEAD_DIM
            dkv = GQA_KV_HEADS * HEAD_DIM
            q = _proj_qk_rope(a, gqa_w_q, j, 0, dq, gqa_q_norm[j], cos_t, sin_t, HEAD_DIM ** -0.5 * LOG2_E, "gqa_q")
            k = _proj_qk_rope(a, gqa_w_kv, j, 0, dkv, gqa_k_norm[j], cos_t, sin_t, 1.0, "gqa_k")
            v = _proj_plain(a, gqa_w_kv, j, dkv, dkv, "gqa_v")
            o = _gqa_attention(q, k, v, n_ctx)
            h = _proj_resid(o, gqa_w_out, j, h, gate1, n_ctx, "gqa_out")
        gate2 = mod[i, :, 5]
        if i % 2 == 0:
            f = _norm_mod(h, norm_ffn[i], mod[i, :, 3:5], n_ctx)
            act = _proj_gu(f, ffn_w_gu, i // 2)
            h = _proj_resid(act, ffn_w_down, i // 2, h, gate2, n_ctx, "ffn_down")
        else:
            h = _moe(h, norm_ffn[i], mod[i, :, 3:5], gate2, moe_w_router, moe_w_gu, moe_w_down, i // 2, n_ctx)
    return _final_norm(h, norm_final)[None]
```

```python
import functools

import numpy as np
import jax
import jax.numpy as jnp
from jax import lax
from jax.experimental import pallas as pl
from jax.experimental.pallas import tpu as pltpu

F32 = jnp.float32
BF16 = jnp.bfloat16
I32 = jnp.int32

EPS = 1e-6
GRID_W = 64
N_MIXERS = 3
NA_HEADS = 16
NA_KH = 8
NA_KW = 16
NA_QROWS = 4
NA_WROWS = NA_QROWS + NA_KH
GQA_HEADS = 16
GQA_KV_HEADS = 4
HEAD_DIM = 128
ROPE_THETA = 10000.0
N_EXPERTS = 8
TOP_K = 2
LANES = 128
ROW_BLOCK = 256
MASKED = -1e30
LOG2_E = float(np.log2(np.e))
VMEM_LIMIT = 60 * 1024 * 1024


def _cparams(*sem):
    return pltpu.CompilerParams(dimension_semantics=sem, vmem_limit_bytes=VMEM_LIMIT)


def _pick(n, cands):
    for c in cands:
        if n % c == 0:
            return c
    raise ValueError(f"no tile for {n} in {cands}")


def _dot(a, b):
    return jnp.dot(a, b, preferred_element_type=F32)


def _dot_nt(a, b):
    return lax.dot_general(a, b, (((1,), (1,)), ((), ())), preferred_element_type=F32)


def _silu(x):
    return x * (1.0 / (1.0 + jnp.exp(-x)))


def _rms(x, gain):
    return x * lax.rsqrt(jnp.mean(x * x, axis=-1, keepdims=True) + EPS) * gain


def _ada_kernel(c_ref, w_ref, b_ref, o_ref):
    s = _silu(c_ref[...])
    o_ref[...] = jnp.dot(s, w_ref[...], precision=lax.Precision.HIGHEST,
                         preferred_element_type=F32) + b_ref[...]


def _ada(cvec, ada_w, ada_b):
    depth, d, n = ada_w.shape
    tn = _pick(n, (1024, 512, 256, 128))
    return pl.pallas_call(
        _ada_kernel,
        grid=(depth, n // tn),
        in_specs=[pl.BlockSpec((8, d), lambda i, j: (0, 0)),
                  pl.BlockSpec((None, d, tn), lambda i, j: (i, 0, j)),
                  pl.BlockSpec((None, 1, tn), lambda i, j: (i, 0, j))],
        out_specs=pl.BlockSpec((None, 8, tn), lambda i, j: (i, 0, j)),
        out_shape=jax.ShapeDtypeStruct((depth, 8, n), F32),
        compiler_params=_cparams("arbitrary", "arbitrary"),
        name="ada",
    )(cvec, ada_w, ada_b.reshape(depth, 1, n))


def _norm_mod_kernel(h_ref, g_ref, ss_ref, a_ref):
    y = _rms(h_ref[...], g_ref[...])
    a_ref[...] = (y * (1.0 + ss_ref[1:2, :]) + ss_ref[0:1, :]).astype(a_ref.dtype)


def _seg_map(n_ctx_blocks):
    return lambda i: (jnp.where(i < n_ctx_blocks, 0, 1), 0, 0)


def _norm_mod(h, gain, ss, n_ctx):
    m, d = h.shape
    tm = ROW_BLOCK
    return pl.pallas_call(
        _norm_mod_kernel,
        grid=(m // tm,),
        in_specs=[pl.BlockSpec((tm, d), lambda i: (i, 0)),
                  pl.BlockSpec((1, d), lambda i: (0, 0)),
                  pl.BlockSpec((None, 2, d), _seg_map(n_ctx // tm))],
        out_specs=pl.BlockSpec((tm, d), lambda i: (i, 0)),
        out_shape=jax.ShapeDtypeStruct((m, d), BF16),
        compiler_params=_cparams("arbitrary"),
        name="norm_mod",
    )(h, gain.reshape(1, d), ss)


def _router_kernel(h_ref, g_ref, ss_ref, wr_ref, ids_ref, wts_ref):
    y = _rms(h_ref[...], g_ref[...])
    f = y * (1.0 + ss_ref[1:2, :]) + ss_ref[0:1, :]
    logits = jnp.dot(f, wr_ref[...], precision=lax.Precision.HIGHEST, preferred_element_type=F32)
    lane = lax.broadcasted_iota(I32, logits.shape, 1)
    neg = jnp.float32(-jnp.inf)
    l1 = jnp.where(lane < N_EXPERTS, logits, neg)
    m1 = jnp.max(l1, axis=-1, keepdims=True)
    i1 = jnp.min(jnp.where(l1 == m1, lane, LANES), axis=-1, keepdims=True)
    l2 = jnp.where(lane == i1, neg, l1)
    m2 = jnp.max(l2, axis=-1, keepdims=True)
    i2 = jnp.min(jnp.where(l2 == m2, lane, LANES), axis=-1, keepdims=True)
    e2 = jnp.exp(m2 - m1)
    w1 = 1.0 / (1.0 + e2)
    w2 = e2 / (1.0 + e2)
    ids_ref[...] = jnp.where(lane == 0, i1, jnp.where(lane == 1, i2, 0))
    wts_ref[...] = jnp.where(lane == 0, w1, jnp.where(lane == 1, w2, 0.0))


def _router(h, gain, ss, w_router, n_ctx):
    m, d = h.shape
    tm = ROW_BLOCK
    wr = jnp.zeros((d, LANES), F32).at[:, :N_EXPERTS].set(w_router)
    return pl.pallas_call(
        _router_kernel,
        grid=(m // tm,),
        in_specs=[pl.BlockSpec((tm, d), lambda i: (i, 0)),
                  pl.BlockSpec((1, d), lambda i: (0, 0)),
                  pl.BlockSpec((None, 2, d), _seg_map(n_ctx // tm)),
                  pl.BlockSpec((d, LANES), lambda i: (0, 0))],
        out_specs=[pl.BlockSpec((tm, LANES), lambda i: (i, 0)),
                   pl.BlockSpec((tm, LANES), lambda i: (i, 0))],
        out_shape=[jax.ShapeDtypeStruct((m, LANES), I32),
                   jax.ShapeDtypeStruct((m, LANES), F32)],
        compiler_params=_cparams("arbitrary"),
        name="router",
    )(h, gain.reshape(1, d), ss, wr)


def _final_norm_kernel(h_ref, g_ref, o_ref):
    o_ref[...] = _rms(h_ref[...], g_ref[...])


def _final_norm(h, gain):
    m, d = h.shape
    tm = ROW_BLOCK
    return pl.pallas_call(
        _final_norm_kernel,
        grid=(m // tm,),
        in_specs=[pl.BlockSpec((tm, d), lambda i: (i, 0)),
                  pl.BlockSpec((1, d), lambda i: (0, 0))],
        out_specs=pl.BlockSpec((tm, d), lambda i: (i, 0)),
        out_shape=jax.ShapeDtypeStruct((m, d), F32),
        compiler_params=_cparams("arbitrary"),
        name="final_norm",
    )(h, gain.reshape(1, d))


def _mm(x, w, *, sel, col_offs, n_tiles, tn, tm, epi, extra=(), extra_specs=(), out_shape, out_specs, name):
    m_rows, k = x.shape
    n_w = len(col_offs)
    n_ex = len(extra)
    n_out = len(out_shape)

    def kern(*refs):
        x_ref = refs[0]
        w_refs = refs[1:1 + n_w]
        ex = refs[1 + n_w:1 + n_w + n_ex]
        outs = refs[1 + n_w + n_ex:1 + n_w + n_ex + n_out]
        wb = refs[-1]
        m = pl.program_id(1)

        @pl.when(m == 0)
        def _():
            for j in range(n_w):
                wb[j] = w_refs[j][...].astype(BF16)

        xv = x_ref[...]
        accs = [_dot(xv, wb[j]) for j in range(n_w)]
        for o, r in zip(outs, epi(accs, m, *ex)):
            o[...] = r.astype(o.dtype)

    def w_map(off):
        return lambda n, m: (sel, 0, n + off)

    in_specs = ([pl.BlockSpec((tm, k), lambda n, m: (m, 0))]
                + [pl.BlockSpec((None, k, tn), w_map(off)) for off in col_offs]
                + list(extra_specs))
    return pl.pallas_call(kern, grid=(n_tiles, m_rows // tm), in_specs=in_specs, out_specs=out_specs,
                          out_shape=out_shape, scratch_shapes=[pltpu.VMEM((n_w, k, tn), BF16)],
                          compiler_params=_cparams("arbitrary", "arbitrary"), name=name)(x, *([w] * n_w), *extra)


X_AHEAD = 3


def _mm_grouped(x, w, starts, counts, slab0, *, col_offs, n_tiles, tn, epi, out_dtype, n_out_cols, name):
    n_rows, k = x.shape
    n_w = len(col_offs)
    ts = MOE_TILE

    def kern(st_ref, ct_ref, s0_ref, x_hbm, *rest):
        w_refs = rest[:n_w]
        o_hbm = rest[n_w]
        xbuf, obuf, wb, pend, sin, sout = rest[n_w + 1:]
        n, e = pl.program_id(0), pl.program_id(1)
        base, cnt = st_ref[e], ct_ref[e]

        def x_copy_from(first_row, r, slot):
            row0 = pl.multiple_of(first_row + r * ts, ts)
            return pltpu.make_async_copy(x_hbm.at[pl.ds(row0, ts), :], xbuf.at[slot], sin.at[slot])

        def x_copy(r, slot):
            return x_copy_from(base, r, slot)

        def x_start_head(first_row, n_tiles_here, cond):
            for r0 in range(X_AHEAD):
                @pl.when(jnp.logical_and(cond, r0 < n_tiles_here))
                def _():
                    x_copy_from(first_row, r0, r0).start()

        def o_copy(r, slot):
            row0 = pl.multiple_of(base + r * ts, ts)
            col0 = pl.multiple_of(n * tn, tn)
            return pltpu.make_async_copy(obuf.at[slot], o_hbm.at[pl.ds(row0, ts), pl.ds(col0, tn)], sout.at[slot])

        def o_drain(slot):
            @pl.when(pend[slot] == 1)
            def _():
                o_copy(0, slot).wait()
                pend[slot] = 0

        def o_emit(r, slot, tile):
            o_drain(slot)
            obuf[slot] = tile
            o_copy(r, slot).start()
            pend[slot] = 1

        first_step = jnp.logical_and(n == 0, e == 0)
        last_step = jnp.logical_and(n == pl.num_programs(0) - 1, e == pl.num_programs(1) - 1)

        @pl.when(first_step)
        def _():
            pend[0] = 0
            pend[1] = 0

        x_start_head(base, cnt, first_step)

        for j in range(n_w):
            wb[j] = w_refs[j][...].astype(BF16)

        def body(r, carry):
            slot = r % (X_AHEAD + 1)

            @pl.when(r + X_AHEAD < cnt)
            def _():
                x_copy(r + X_AHEAD, (r + X_AHEAD) % (X_AHEAD + 1)).start()

            x_copy(r, slot).wait()
            xv = xbuf[slot]
            o_emit(r, r % 2, epi([_dot(xv, wb[j]) for j in range(n_w)]).astype(out_dtype))
            return carry

        lax.fori_loop(0, cnt, body, 0)

        e_next = jnp.where(e == pl.num_programs(1) - 1, 0, e + 1)
        x_start_head(st_ref[e_next], ct_ref[e_next], jnp.logical_not(last_step))

        @pl.when(e == pl.num_programs(1) - 1)
        def _():
            def zero_tile(r, carry):
                o_emit(r, r % 2, jnp.zeros((ts, tn), out_dtype))
                return carry

            lax.fori_loop(cnt, (n_rows - base) // ts, zero_tile, 0)

        @pl.when(last_step)
        def _():
            o_drain(0)
            o_drain(1)

    def w_map(off):
        return lambda n, e, st, ct, s0: (s0[0] + e, 0, n + off)

    gs = pltpu.PrefetchScalarGridSpec(
        num_scalar_prefetch=3,
        grid=(n_tiles, N_EXPERTS),
        in_specs=[pl.BlockSpec(memory_space=pl.ANY)] + [pl.BlockSpec((None, k, tn), w_map(off)) for off in col_offs],
        out_specs=pl.BlockSpec(memory_space=pl.ANY),
        scratch_shapes=[pltpu.VMEM((X_AHEAD + 1, ts, k), BF16), pltpu.VMEM((2, ts, tn), out_dtype),
                        pltpu.VMEM((n_w, k, tn), BF16), pltpu.SMEM((2,), I32),
                        pltpu.SemaphoreType.DMA((X_AHEAD + 1,)), pltpu.SemaphoreType.DMA((2,))])
    return pl.pallas_call(kern, grid_spec=gs, out_shape=jax.ShapeDtypeStruct((n_rows, n_out_cols), out_dtype),
                          compiler_params=_cparams("arbitrary", "arbitrary"),
                          name=name)(starts, counts, slab0, x, *([w] * n_w))


def _row_tile(m_rows):
    return _pick(m_rows, (768, 512, 256))


def _proj_plain(x, w, sel, col0, n_cols, name):
    m_rows = x.shape[0]
    tn = _pick(n_cols, (1024, 512, 256))
    assert col0 % tn == 0
    tm = _row_tile(m_rows)
    return _mm(x, w, sel=sel, col_offs=(col0 // tn,), n_tiles=n_cols // tn, tn=tn, tm=tm,
               epi=lambda accs, m: (accs[0],),
               out_shape=[jax.ShapeDtypeStruct((m_rows, n_cols), BF16)],
               out_specs=[pl.BlockSpec((tm, tn), lambda n, m: (m, n))], name=name)[0]


def _proj_conv_in(x, w, sel):
    m_rows = x.shape[0]
    d3 = w.shape[2] // 3
    tn = _pick(d3, (512, 256))
    tm = _row_tile(m_rows)
    nt = d3 // tn
    o = jax.ShapeDtypeStruct((m_rows, d3), BF16)
    spec = pl.BlockSpec((tm, tn), lambda n, m: (m, n))
    return _mm(x, w, sel=sel, col_offs=(0, nt, 2 * nt), n_tiles=nt, tn=tn, tm=tm,
               epi=lambda accs, m: (accs[0], accs[1] * accs[2]),
               out_shape=[o, o], out_specs=[spec, spec], name="conv_in")


def _proj_gu(x, w, sel):
    m_rows = x.shape[0]
    f = w.shape[2] // 2
    tn = _pick(f, (512, 256))
    tm = _row_tile(m_rows)
    nt = f // tn
    return _mm(x, w, sel=sel, col_offs=(0, nt), n_tiles=nt, tn=tn, tm=tm,
               epi=lambda accs, m: (_silu(accs[0]) * accs[1],),
               out_shape=[jax.ShapeDtypeStruct((m_rows, f), BF16)],
               out_specs=[pl.BlockSpec((tm, tn), lambda n, m: (m, n))], name="ffn_gu")[0]


def _moe_gu(x, w, starts, counts, slab0):
    f = w.shape[2] // 2
    tn = _pick(f, (512, 256))
    nt = f // tn
    return _mm_grouped(x, w, starts, counts, slab0, col_offs=(0, nt), n_tiles=nt, tn=tn,
                       epi=lambda accs: _silu(accs[0]) * accs[1], out_dtype=BF16, n_out_cols=f, name="moe_gu")


def _moe_down(x, w, starts, counts, slab0):
    n_cols = w.shape[2]
    tn = 512
    return _mm_grouped(x, w, starts, counts, slab0, col_offs=(0,), n_tiles=n_cols // tn, tn=tn,
                       epi=lambda accs: accs[0], out_dtype=F32, n_out_cols=n_cols, name="moe_down")


def _proj_qk_rope(x, w, sel, col0, n_cols, gain, cos_t, sin_t, scale, name):
    m_rows = x.shape[0]
    tn = _pick(n_cols, (512, 256, 128))
    assert col0 % tn == 0
    tm = _row_tile(m_rows)

    def epi(accs, m, gain_ref, cos_ref, sin_ref):
        lane = lax.broadcasted_iota(I32, (1, HEAD_DIM), 1)
        first = (lane & (HEAD_DIM // 4)) == 0
        cos_v, sin_v, g = cos_ref[...], sin_ref[...], gain_ref[...]
        heads = []
        for hh in range(tn // HEAD_DIM):
            y = _rms(accs[0][:, hh * HEAD_DIM:(hh + 1) * HEAD_DIM], g)
            rot = jnp.where(first, pltpu.roll(y, HEAD_DIM - HEAD_DIM // 4, 1), pltpu.roll(y, HEAD_DIM // 4, 1))
            heads.append((y * cos_v + rot * sin_v) * scale)
        return (jnp.concatenate(heads, axis=1),)

    return _mm(x, w, sel=sel, col_offs=(col0 // tn,), n_tiles=n_cols // tn, tn=tn, tm=tm, epi=epi,
               extra=(gain.reshape(1, HEAD_DIM), cos_t, sin_t),
               extra_specs=(pl.BlockSpec((1, HEAD_DIM), lambda n, m: (0, 0)),
                            pl.BlockSpec((tm, HEAD_DIM), lambda n, m: (m, 0)),
                            pl.BlockSpec((tm, HEAD_DIM), lambda n, m: (m, 0))),
               out_shape=[jax.ShapeDtypeStruct((m_rows, n_cols), BF16)],
               out_specs=[pl.BlockSpec((tm, tn), lambda n, m: (m, n))], name=name)[0]


def _proj_resid(x, w, sel, h, gate, n_ctx, name):
    m_rows, k = x.shape
    n_cols = w.shape[2]
    big_k = k > 4096
    tn = 512 if big_k else _pick(n_cols, (1024, 512, 256))
    tm = _pick(m_rows, (512, 384, 256)) if big_k else _row_tile(m_rows)

    def epi(accs, m, h_ref, gate_ref):
        row = m * tm + lax.broadcasted_iota(I32, (tm, 1), 0)
        g = jnp.where(row < n_ctx, gate_ref[0:1, :], gate_ref[1:2, :])
        return (h_ref[...] + g * accs[0],)

    return _mm(x, w, sel=sel, col_offs=(0,), n_tiles=n_cols // tn, tn=tn, tm=tm, epi=epi,
               extra=(h, gate),
               extra_specs=(pl.BlockSpec((tm, tn), lambda n, m: (m, n)),
                            pl.BlockSpec((2, tn), lambda n, m: (0, n))),
               out_shape=[jax.ShapeDtypeStruct((m_rows, n_cols), F32)],
               out_specs=[pl.BlockSpec((tm, tn), lambda n, m: (m, n))], name=name)[0]


def _conv_gate_kernel(b_ref, u_ref, up_ref, un_ref, cw_ref, z_ref, *, tm, n_ctx_blocks, n_blocks):
    i = pl.program_id(0)
    u = u_ref[...].astype(F32)
    prev_ok = jnp.logical_and(i != 0, i != n_ctx_blocks).astype(F32)
    next_ok = jnp.logical_and(i != n_ctx_blocks - 1, i != n_blocks - 1).astype(F32)
    prow = up_ref[15:16, :].astype(F32) * prev_ok
    nrow = un_ref[0:1, :].astype(F32) * next_ok
    row = lax.broadcasted_iota(I32, (tm, 1), 0)
    um1 = jnp.where(row == 0, prow, pltpu.roll(u, 1, 0))
    up1 = jnp.where(row == tm - 1, nrow, pltpu.roll(u, tm - 1, 0))
    conv = cw_ref[0:1, :] * um1 + cw_ref[1:2, :] * u + cw_ref[2:3, :] * up1
    z_ref[...] = (b_ref[...].astype(F32) * conv).astype(z_ref.dtype)


def _conv_gate(b, u, conv_w, n_ctx):
    m, d = u.shape
    tm = ROW_BLOCK
    tc = _pick(d, (1024, 512, 256, 128))
    hb = 16
    nb = m // tm
    last_hb = m // hb - 1
    kern = functools.partial(_conv_gate_kernel, tm=tm, n_ctx_blocks=n_ctx // tm, n_blocks=nb)
    return pl.pallas_call(
        kern,
        grid=(nb, d // tc),
        in_specs=[pl.BlockSpec((tm, tc), lambda i, j: (i, j)),
                  pl.BlockSpec((tm, tc), lambda i, j: (i, j)),
                  pl.BlockSpec((hb, tc), lambda i, j: (jnp.maximum(i * (tm // hb) - 1, 0), j)),
                  pl.BlockSpec((hb, tc), lambda i, j: (jnp.minimum((i + 1) * (tm // hb), last_hb), j)),
                  pl.BlockSpec((3, tc), lambda i, j: (0, j))],
        out_specs=pl.BlockSpec((tm, tc), lambda i, j: (i, j)),
        out_shape=jax.ShapeDtypeStruct((m, d), BF16),
        compiler_params=_cparams("arbitrary", "arbitrary"),
        name="conv_gate",
    )(b, u, u, u, conv_w)


def _na_bias_tables(rpb, rows):
    nb = rows // NA_QROWS
    n_h = rpb.shape[0]
    c = np.arange(GRID_W)[:, None]
    kc = np.arange(GRID_W)[None, :]
    c0 = np.clip(c - NA_KW // 2, 0, GRID_W - NA_KW)
    c_valid = (kc >= c0) & (kc < c0 + NA_KW)
    c_sel = (kc - c + NA_KW - 1)[:, :, None] == np.arange(2 * NA_KW - 1)[None, None, :]
    c_sel = (c_sel & c_valid[:, :, None]).astype(np.float32)
    r_sel, r_valid = [], []
    for b in (0, 1, nb - 1):
        w0 = NA_QROWS * int(np.clip(b - 1, 0, nb - 3))
        ar = NA_QROWS * b + np.arange(NA_QROWS)[:, None]
        kr = w0 + np.arange(NA_WROWS)[None, :]
        r0 = np.clip(ar - NA_KH // 2, 0, rows - NA_KH)
        ok = (kr >= r0) & (kr < r0 + NA_KH)
        sel = (kr - ar + NA_KH - 1)[:, :, None] == np.arange(2 * NA_KH - 1)[None, None, :]
        r_sel.append((sel & ok[:, :, None]).astype(np.float32))
        r_valid.append(ok)
    r_sel, r_valid = np.stack(r_sel), np.stack(r_valid)
    t = jnp.einsum("pijr,hrd,ckd->phicjk", jnp.asarray(r_sel), rpb, jnp.asarray(c_sel),
                   precision=lax.Precision.HIGHEST)
    valid = r_valid[:, None, :, None, :, None] & c_valid[None, None, None, :, None, :]
    t = jnp.where(valid, t, MASKED).reshape(3, n_h, NA_QROWS * GRID_W, NA_WROWS * GRID_W)
    return jnp.concatenate([jnp.full((1,) + t.shape[1:], MASKED, F32), t], axis=0)


def _na_kernel(q_ref, k0_ref, k1_ref, k2_ref, kc_ref, v0_ref, v1_ref, v2_ref, vc_ref, bias_ref, o_ref, *, scale):
    blk = ROW_BLOCK
    for h in range(NA_HEADS):
        hs = slice(h * HEAD_DIM, (h + 1) * HEAD_DIM)
        q = q_ref[:, hs]
        ss = [_dot_nt(q, kr[:, hs]) * scale + bias_ref[h, :, j * blk:(j + 1) * blk]
              for j, kr in enumerate((k0_ref, k1_ref, k2_ref))]
        ss.append(_dot_nt(q, kc_ref[:, hs]) * scale)
        mx = functools.reduce(jnp.maximum, [jnp.max(s, axis=-1, keepdims=True) for s in ss])
        ps = [jnp.exp(s - mx) for s in ss]
        den = functools.reduce(jnp.add, [jnp.sum(p, axis=-1, keepdims=True) for p in ps])
        acc = functools.reduce(jnp.add, [_dot(p.astype(BF16), vr[:, hs])
                                         for p, vr in zip(ps, (v0_ref, v1_ref, v2_ref, vc_ref))])
        o_ref[:, hs] = (acc / den).astype(o_ref.dtype)


def _na_attention(qkv, bias, n_ctx):
    m, d3 = qkv.shape
    d = d3 // 3
    blk = ROW_BLOCK
    assert n_ctx == blk and NA_QROWS * GRID_W == blk
    nq = m // blk
    nb = nq - 1

    def kv_map(j, col):
        return lambda g: (1 + jnp.clip(g - 2, 0, nb - 3) + j, col)

    def bias_map(g):
        return (jnp.where(g == 0, 0, jnp.where(g == 1, 1, jnp.where(g == nq - 1, 3, 2))), 0, 0, 0)

    blkspec = lambda imap: pl.BlockSpec((blk, d), imap)
    return pl.pallas_call(
        functools.partial(_na_kernel, scale=HEAD_DIM ** -0.5),
        grid=(nq,),
        in_specs=[blkspec(lambda g: (g, 0)),
                  blkspec(kv_map(0, 1)), blkspec(kv_map(1, 1)), blkspec(kv_map(2, 1)), blkspec(lambda g: (0, 1)),
                  blkspec(kv_map(0, 2)), blkspec(kv_map(1, 2)), blkspec(kv_map(2, 2)), blkspec(lambda g: (0, 2)),
                  pl.BlockSpec((None, NA_HEADS, blk, 3 * blk), bias_map, pipeline_mode=pl.Buffered(1))],
        out_specs=blkspec(lambda g: (g, 0)),
        out_shape=jax.ShapeDtypeStruct((m, d), BF16),
        compiler_params=_cparams("arbitrary"),
        name="na_attention",
    )(qkv, qkv, qkv, qkv, qkv, qkv, qkv, qkv, qkv, bias)


def _rope_tables(n_ctx, n_lat):
    t = jnp.arange(n_lat)
    row = (t // GRID_W).astype(F32)
    col = (t % GRID_W).astype(F32)
    quarter = HEAD_DIM // 4
    inv_freq = ROPE_THETA ** (-jnp.arange(quarter, dtype=F32) / quarter)
    ar, ac = row[:, None] * inv_freq, col[:, None] * inv_freq
    cos_t = jnp.concatenate([jnp.cos(ar), jnp.cos(ar), jnp.cos(ac), jnp.cos(ac)], axis=1)
    sin_t = jnp.concatenate([-jnp.sin(ar), jnp.sin(ar), -jnp.sin(ac), jnp.sin(ac)], axis=1)
    cos_t = jnp.concatenate([jnp.ones((n_ctx, HEAD_DIM), F32), cos_t], axis=0)
    sin_t = jnp.concatenate([jnp.zeros((n_ctx, HEAD_DIM), F32), sin_t], axis=0)
    return cos_t, sin_t


GQA_SUBCHUNKS = 32


def _gqa_kernel(q_ref, k_ref, v_ref, o_ref, *, n_ctx, tk, n_lat_chunks, group):
    is_ctx = pl.program_id(1) * ROW_BLOCK < n_ctx
    n_it = jnp.where(is_ctx, 0, n_lat_chunks)
    qs = [q_ref[:, g * HEAD_DIM:(g + 1) * HEAD_DIM] for g in range(group)]
    kc, vc = k_ref[0:n_ctx, :], v_ref[0:n_ctx, :]
    state = []
    for g in range(group):
        s = _dot_nt(qs[g], kc)
        mx = jnp.max(s, axis=-1, keepdims=True)
        state += [mx, _dot(jnp.exp2(s - mx).astype(BF16), vc)]

    def body(c, carry):
        out = list(carry)
        for sub in range(GQA_SUBCHUNKS):
            off = pl.multiple_of(n_ctx + c * tk + sub * (tk // GQA_SUBCHUNKS), ROW_BLOCK)
            kk = k_ref[pl.ds(off, tk // GQA_SUBCHUNKS), :]
            vv = v_ref[pl.ds(off, tk // GQA_SUBCHUNKS), :]
            for g in range(group):
                mx, acc = out[2 * g:2 * g + 2]
                s = _dot_nt(qs[g], kk)
                mx_new = jnp.maximum(mx, jnp.max(s, axis=-1, keepdims=True))
                out[2 * g] = mx_new
                out[2 * g + 1] = jnp.exp2(mx - mx_new) * acc + _dot(jnp.exp2(s - mx_new).astype(BF16), vv)
        return tuple(out)

    state = lax.fori_loop(0, n_it, body, tuple(state))
    for g in range(group):
        acc = state[2 * g + 1]
        o_ref[:, g * HEAD_DIM:(g + 1) * HEAD_DIM] = (acc[:, :HEAD_DIM] / acc[:, HEAD_DIM:]).astype(o_ref.dtype)


def _gqa_attention(q, k, v, n_ctx):
    m, dq = q.shape
    group = GQA_HEADS // GQA_KV_HEADS
    gw = group * HEAD_DIM
    tq = ROW_BLOCK
    n_lat = m - n_ctx
    tk = _pick(n_lat, (8192, 4096, 2048, 1024, 512, 256))
    kern = functools.partial(_gqa_kernel, n_ctx=n_ctx, tk=tk, n_lat_chunks=n_lat // tk, group=group)
    v = jnp.concatenate([v.reshape(m, GQA_KV_HEADS, HEAD_DIM), jnp.ones((m, GQA_KV_HEADS, HEAD_DIM), v.dtype)],
                        axis=2).reshape(m, 2 * GQA_KV_HEADS * HEAD_DIM)
    return pl.pallas_call(
        kern,
        grid=(GQA_KV_HEADS, m // tq),
        in_specs=[pl.BlockSpec((tq, gw), lambda kh, i: (i, kh)),
                  pl.BlockSpec((m, HEAD_DIM), lambda kh, i: (0, kh)),
                  pl.BlockSpec((m, 2 * HEAD_DIM), lambda kh, i: (0, kh))],
        out_specs=pl.BlockSpec((tq, gw), lambda kh, i: (i, kh)),
        out_shape=jax.ShapeDtypeStruct((m, dq), BF16),
        compiler_params=_cparams("arbitrary", "arbitrary"),
        name="gqa_attention",
    )(q, k, v)


MOE_TILE = 256


def _moe_plan(ids):
    m = ids.shape[0]
    tm = MOE_TILE
    n_rows = -(-(TOP_K * m + N_EXPERTS * (tm - 1)) // tm) * tm
    flat_e = ids.reshape(-1)
    onehot = (flat_e[:, None] == jnp.arange(N_EXPERTS, dtype=I32)[None, :]).astype(I32)
    csum = jnp.cumsum(onehot, axis=0)
    rank = jnp.sum(csum * onehot, axis=1) - 1
    tiles = (csum[-1] + tm - 1) // tm
    starts = (jnp.cumsum(tiles) - tiles) * tm
    pos = jnp.sum(starts[None, :] * onehot, axis=1) + rank
    tok = jnp.arange(TOP_K * m, dtype=I32) // TOP_K
    src = jnp.zeros((n_rows,), I32).at[pos].set(tok)
    return src, pos.astype(I32), starts.astype(I32), tiles.astype(I32)


ROW_UNROLL = 8


def _prefetch_rows(idx_ref, src_hbm, buf, sem, i, n_steps, n_rows):
    def row_copy(slot, r, row):
        return pltpu.make_async_copy(src_hbm.at[pl.ds(row, 1), :], buf.at[slot, pl.ds(r, 1), :], sem.at[slot])

    def start_all(step):
        slot = step % 2

        def start(r8, carry):
            for u in range(ROW_UNROLL):
                r = r8 * ROW_UNROLL + u
                row_copy(slot, r, idx_ref[step * n_rows + r]).start(priority=u % 2)
            return carry

        lax.fori_loop(0, n_rows // ROW_UNROLL, start, 0)

    @pl.when(i == 0)
    def _():
        start_all(i)

    @pl.when(i + 1 < n_steps)
    def _():
        start_all(i + 1)

    def wait(r, carry):
        row_copy(i % 2, r, 0).wait()
        return carry

    lax.fori_loop(0, n_rows, wait, 0, unroll=8)


def _gather_norm_kernel(src_ref, tok_ref, h_hbm, g_ref, ssc_ref, ssl_ref, o_ref, buf, sem, *, tg, n_ctx):
    i = pl.program_id(0)
    _prefetch_rows(src_ref, h_hbm, buf, sem, i, pl.num_programs(0), tg)
    y = _rms(buf[i % 2], g_ref[...])
    scale, shift = ssl_ref[1:2, :], ssl_ref[0:1, :]
    if n_ctx > 0:
        is_ctx = tok_ref[...] < n_ctx
        scale = jnp.where(is_ctx, ssc_ref[1:2, :], scale)
        shift = jnp.where(is_ctx, ssc_ref[0:1, :], shift)
    o_ref[...] = (y * (1.0 + scale) + shift).astype(o_ref.dtype)


def _gather_norm(h, src, gain, ss, n_ctx):
    m, d = h.shape
    n_rows = src.shape[0]
    tg = ROW_BLOCK
    kern = functools.partial(_gather_norm_kernel, tg=tg, n_ctx=n_ctx)
    gs = pltpu.PrefetchScalarGridSpec(
        num_scalar_prefetch=1,
        grid=(n_rows // tg,),
        in_specs=[pl.BlockSpec((tg, 1), lambda i, *pf: (i, 0)),
                  pl.BlockSpec(memory_space=pl.ANY),
                  pl.BlockSpec((1, d), lambda i, *pf: (0, 0)),
                  pl.BlockSpec((None, 2, d), lambda i, *pf: (0, 0, 0)),
                  pl.BlockSpec((None, 2, d), lambda i, *pf: (1, 0, 0))],
        out_specs=pl.BlockSpec((tg, d), lambda i, *pf: (i, 0)),
        scratch_shapes=[pltpu.VMEM((2, tg, d), F32), pltpu.SemaphoreType.DMA((2,))])
    return pl.pallas_call(kern, grid_spec=gs, out_shape=jax.ShapeDtypeStruct((n_rows, d), BF16),
                          compiler_params=_cparams("arbitrary"),
                          name="moe_gather")(src, src.reshape(n_rows, 1), h, gain.reshape(1, d), ss, ss)


def _combine_kernel(pos_ref, h_ref, gate_ref, wts_ref, y_hbm, o_ref, buf, sem, *, tg, n_ctx):
    i = pl.program_id(0)
    _prefetch_rows(pos_ref, y_hbm, buf, sem, i, pl.num_programs(0), tg * TOP_K)
    row = i * tg + lax.broadcasted_iota(I32, (tg, 1), 0)
    g = jnp.where(row < n_ctx, gate_ref[0:1, :], gate_ref[1:2, :])
    slot = i % 2
    mix = wts_ref[:, 0:1] * buf[slot, 0:tg, :] + wts_ref[:, 1:2] * buf[slot, tg:TOP_K * tg, :]
    o_ref[...] = h_ref[...] + g * mix


def _combine(h, y, pos, wts, gate, n_ctx):
    m, d = h.shape
    tg = ROW_BLOCK
    pos_tiles = pos.reshape(m // tg, tg, TOP_K).transpose(0, 2, 1).reshape(-1)
    kern = functools.partial(_combine_kernel, tg=tg, n_ctx=n_ctx)
    gs = pltpu.PrefetchScalarGridSpec(
        num_scalar_prefetch=1,
        grid=(m // tg,),
        in_specs=[pl.BlockSpec((tg, d), lambda i, *pf: (i, 0)),
                  pl.BlockSpec((2, d), lambda i, *pf: (0, 0)),
                  pl.BlockSpec((tg, LANES), lambda i, *pf: (i, 0)),
                  pl.BlockSpec(memory_space=pl.ANY)],
        out_specs=pl.BlockSpec((tg, d), lambda i, *pf: (i, 0)),
        scratch_shapes=[pltpu.VMEM((2, TOP_K * tg, d), F32), pltpu.SemaphoreType.DMA((2,))])
    return pl.pallas_call(kern, grid_spec=gs, out_shape=jax.ShapeDtypeStruct((m, d), F32),
                          compiler_params=_cparams("arbitrary"), name="moe_combine")(pos_tiles, h, gate, wts, y)


def _moe(h, gain, ss, gate, w_router, w_gu, w_down, j, n_ctx):
    ids, wts = _router(h, gain, ss, w_router[j], n_ctx)
    src, pos, starts, tiles = _moe_plan(ids[:, :TOP_K])
    xs = _gather_norm(h, src, gain, ss, n_ctx)
    slab0 = jnp.full((1,), j * N_EXPERTS, I32)
    act = _moe_gu(xs, w_gu.reshape((-1,) + w_gu.shape[2:]), starts, tiles, slab0)
    y = _moe_down(act, w_down.reshape((-1,) + w_down.shape[2:]), starts, tiles, slab0)
    return _combine(h, y, pos, wts, gate, n_ctx)


def kernel(x, c, ctx, c_ctx, ada_w, ada_b, norm_mix, norm_ffn, norm_final, conv_w_in, conv_w, conv_w_out,
           na_w_qkv, na_rpb, na_w_out, gqa_w_q, gqa_w_kv, gqa_q_norm, gqa_k_norm, gqa_w_out, ffn_w_gu,
           ffn_w_down, moe_w_router, moe_w_gu, moe_w_down):
    bsz, n_lat, d = x.shape
    n_ctx_full = ctx.shape[1]
    depth = ada_w.shape[0]
    assert bsz == 1 and n_ctx_full == ROW_BLOCK and n_lat % (GRID_W * NA_QROWS) == 0
    assert n_lat // GRID_W >= NA_WROWS

    cvec = jnp.zeros((8, d), F32).at[0].set(c_ctx).at[1].set(c[0])
    mod = _ada(cvec, ada_w, ada_b)[:, :2].reshape(depth, 2, 6, d)

    h = jnp.concatenate([ctx[0], x[0]], axis=0)
    n_ctx = n_ctx_full
    cos_t, sin_t = _rope_tables(n_ctx_full, n_lat)

    for i in range(depth):
        if i == depth - 1:
            h = h[n_ctx:]
            n_ctx = 0
        kind, j = i % N_MIXERS, i // N_MIXERS
        a = _norm_mod(h, norm_mix[i], mod[i, :, 0:2], n_ctx)
        gate1 = mod[i, :, 2]
        if kind == 0:
            b_gate, u = _proj_conv_in(a, conv_w_in, j)
            z = _conv_gate(b_gate, u, conv_w[j], n_ctx)
            h = _proj_resid(z, conv_w_out, j, h, gate1, n_ctx, "conv_out")
        elif kind == 1:
            qkv = _proj_plain(a, na_w_qkv, j, 0, 3 * d, "na_qkv")
            bias = _na_bias_tables(na_rpb[j], n_lat // GRID_W)
            o = _na_attention(qkv, bias, n_ctx)
            h = _proj_resid(o, na_w_out, j, h, gate1, n_ctx, "na_out")
        else:
            dq = GQA_HEADS * HEAD_DIM
            dkv = GQA_KV_HEADS * HEAD_DIM
            q = _proj_qk_rope(a, gqa_w_q, j, 0, dq, gqa_q_norm[j], cos_t, sin_t, HEAD_DIM ** -0.5 * LOG2_E, "gqa_q")
            k = _proj_qk_rope(a, gqa_w_kv, j, 0, dkv, gqa_k_norm[j], cos_t, sin_t, 1.0, "gqa_k")
            v = _proj_plain(a, gqa_w_kv, j, dkv, dkv, "gqa_v")
            o = _gqa_attention(q, k, v, n_ctx)
            h = _proj_resid(o, gqa_w_out, j, h, gate1, n_ctx, "gqa_out")
        gate2 = mod[i, :, 5]
        if i % 2 == 0:
            f = _norm_mod(h, norm_ffn[i], mod[i, :, 3:5], n_ctx)
            act = _proj_gu(f, ffn_w_gu, i // 2)
            h = _proj_resid(act, ffn_w_down, i // 2, h, gate2, n_ctx, "ffn_down")
        else:
            h = _moe(h, norm_ffn[i], mod[i, :, 3:5], gate2, moe_w_router, moe_w_gu, moe_w_down, i // 2, n_ctx)
    return _final_norm(h, norm_final)[None]
```

```python
import functools

import numpy as np
import jax
import jax.numpy as jnp
from jax import lax
from jax.experimental import pallas as pl
from jax.experimental.pallas import tpu as pltpu

F32 = jnp.float32
BF16 = jnp.bfloat16
I32 = jnp.int32

EPS = 1e-6
GRID_W = 64
N_MIXERS = 3
NA_HEADS = 16
NA_KH = 8
NA_KW = 16
NA_QROWS = 4
NA_WROWS = NA_QROWS + NA_KH
GQA_HEADS = 16
GQA_KV_HEADS = 4
HEAD_DIM = 128
ROPE_THETA = 10000.0
N_EXPERTS = 8
TOP_K = 2
LANES = 128
ROW_BLOCK = 256
MASKED = -1e30
LOG2_E = float(np.log2(np.e))
VMEM_LIMIT = 60 * 1024 * 1024


def _cparams(*sem):
    return pltpu.CompilerParams(dimension_semantics=sem, vmem_limit_bytes=VMEM_LIMIT)


def _pick(n, cands):
    for c in cands:
        if n % c == 0:
            return c
    raise ValueError(f"no tile for {n} in {cands}")


def _dot(a, b):
    return jnp.dot(a, b, preferred_element_type=F32)


def _dot_nt(a, b):
    return lax.dot_general(a, b, (((1,), (1,)), ((), ())), preferred_element_type=F32)


def _silu(x):
    return x * (1.0 / (1.0 + jnp.exp(-x)))


def _rms(x, gain):
    return x * lax.rsqrt(jnp.mean(x * x, axis=-1, keepdims=True) + EPS) * gain


def _ada_kernel(c_ref, w_ref, b_ref, o_ref):
    s = _silu(c_ref[...])
    o_ref[...] = jnp.dot(s, w_ref[...], precision=lax.Precision.HIGHEST,
                         preferred_element_type=F32) + b_ref[...]


def _ada(cvec, ada_w, ada_b):
    depth, d, n = ada_w.shape
    tn = _pick(n, (1024, 512, 256, 128))
    return pl.pallas_call(
        _ada_kernel,
        grid=(depth, n // tn),
        in_specs=[pl.BlockSpec((8, d), lambda i, j: (0, 0)),
                  pl.BlockSpec((None, d, tn), lambda i, j: (i, 0, j)),
                  pl.BlockSpec((None, 1, tn), lambda i, j: (i, 0, j))],
        out_specs=pl.BlockSpec((None, 8, tn), lambda i, j: (i, 0, j)),
        out_shape=jax.ShapeDtypeStruct((depth, 8, n), F32),
        compiler_params=_cparams("arbitrary", "arbitrary"),
        name="ada",
    )(cvec, ada_w, ada_b.reshape(depth, 1, n))


def _norm_mod_kernel(h_ref, g_ref, ss_ref, a_ref):
    y = _rms(h_ref[...], g_ref[...])
    a_ref[...] = (y * (1.0 + ss_ref[1:2, :]) + ss_ref[0:1, :]).astype(a_ref.dtype)


def _seg_map(n_ctx_blocks):
    return lambda i: (jnp.where(i < n_ctx_blocks, 0, 1), 0, 0)


def _norm_mod(h, gain, ss, n_ctx):
    m, d = h.shape
    tm = ROW_BLOCK
    return pl.pallas_call(
        _norm_mod_kernel,
        grid=(m // tm,),
        in_specs=[pl.BlockSpec((tm, d), lambda i: (i, 0)),
                  pl.BlockSpec((1, d), lambda i: (0, 0)),
                  pl.BlockSpec((None, 2, d), _seg_map(n_ctx // tm))],
        out_specs=pl.BlockSpec((tm, d), lambda i: (i, 0)),
        out_shape=jax.ShapeDtypeStruct((m, d), BF16),
        compiler_params=_cparams("arbitrary"),
        name="norm_mod",
    )(h, gain.reshape(1, d), ss)


def _router_kernel(h_ref, g_ref, ss_ref, wr_ref, ids_ref, wts_ref):
    y = _rms(h_ref[...], g_ref[...])
    f = y * (1.0 + ss_ref[1:2, :]) + ss_ref[0:1, :]
    logits = jnp.dot(f, wr_ref[...], precision=lax.Precision.HIGHEST, preferred_element_type=F32)
    lane = lax.broadcasted_iota(I32, logits.shape, 1)
    neg = jnp.float32(-jnp.inf)
    l1 = jnp.where(lane < N_EXPERTS, logits, neg)
    m1 = jnp.max(l1, axis=-1, keepdims=True)
    i1 = jnp.min(jnp.where(l1 == m1, lane, LANES), axis=-1, keepdims=True)
    l2 = jnp.where(lane == i1, neg, l1)
    m2 = jnp.max(l2, axis=-1, keepdims=True)
    i2 = jnp.min(jnp.where(l2 == m2, lane, LANES), axis=-1, keepdims=True)
    e2 = jnp.exp(m2 - m1)
    w1 = 1.0 / (1.0 + e2)
    w2 = e2 / (1.0 + e2)
    ids_ref[...] = jnp.where(lane == 0, i1, jnp.where(lane == 1, i2, 0))
    wts_ref[...] = jnp.where(lane == 0, w1, jnp.where(lane == 1, w2, 0.0))


def _router(h, gain, ss, w_router, n_ctx):
    m, d = h.shape
    tm = ROW_BLOCK
    wr = jnp.zeros((d, LANES), F32).at[:, :N_EXPERTS].set(w_router)
    return pl.pallas_call(
        _router_kernel,
        grid=(m // tm,),
        in_specs=[pl.BlockSpec((tm, d), lambda i: (i, 0)),
                  pl.BlockSpec((1, d), lambda i: (0, 0)),
                  pl.BlockSpec((None, 2, d), _seg_map(n_ctx // tm)),
                  pl.BlockSpec((d, LANES), lambda i: (0, 0))],
        out_specs=[pl.BlockSpec((tm, LANES), lambda i: (i, 0)),
                   pl.BlockSpec((tm, LANES), lambda i: (i, 0))],
        out_shape=[jax.ShapeDtypeStruct((m, LANES), I32),
                   jax.ShapeDtypeStruct((m, LANES), F32)],
        compiler_params=_cparams("arbitrary"),
        name="router",
    )(h, gain.reshape(1, d), ss, wr)


def _final_norm_kernel(h_ref, g_ref, o_ref):
    o_ref[...] = _rms(h_ref[...], g_ref[...])


def _final_norm(h, gain):
    m, d = h.shape
    tm = ROW_BLOCK
    return pl.pallas_call(
        _final_norm_kernel,
        grid=(m // tm,),
        in_specs=[pl.BlockSpec((tm, d), lambda i: (i, 0)),
                  pl.BlockSpec((1, d), lambda i: (0, 0))],
        out_specs=pl.BlockSpec((tm, d), lambda i: (i, 0)),
        out_shape=jax.ShapeDtypeStruct((m, d), F32),
        compiler_params=_cparams("arbitrary"),
        name="final_norm",
    )(h, gain.reshape(1, d))


def _mm(x, w, *, sel, col_offs, n_tiles, tn, tm, epi, extra=(), extra_specs=(), out_shape, out_specs, name):
    m_rows, k = x.shape
    n_w = len(col_offs)
    n_ex = len(extra)
    n_out = len(out_shape)

    def kern(*refs):
        x_ref = refs[0]
        w_refs = refs[1:1 + n_w]
        ex = refs[1 + n_w:1 + n_w + n_ex]
        outs = refs[1 + n_w + n_ex:1 + n_w + n_ex + n_out]
        wb = refs[-1]
        m = pl.program_id(1)

        @pl.when(m == 0)
        def _():
            for j in range(n_w):
                wb[j] = w_refs[j][...].astype(BF16)

        xv = x_ref[...]
        accs = [_dot(xv, wb[j]) for j in range(n_w)]
        for o, r in zip(outs, epi(accs, m, *ex)):
            if isinstance(r, list):
                c0 = 0
                for piece in r:
                    o[:, c0:c0 + piece.shape[1]] = piece.astype(o.dtype)
                    c0 += piece.shape[1]
            else:
                o[...] = r.astype(o.dtype)

    def w_map(off):
        return lambda n, m: (sel, 0, n + off)

    in_specs = ([pl.BlockSpec((tm, k), lambda n, m: (m, 0))]
                + [pl.BlockSpec((None, k, tn), w_map(off)) for off in col_offs]
                + list(extra_specs))
    return pl.pallas_call(kern, grid=(n_tiles, m_rows // tm), in_specs=in_specs, out_specs=out_specs,
                          out_shape=out_shape, scratch_shapes=[pltpu.VMEM((n_w, k, tn), BF16)],
                          compiler_params=_cparams("arbitrary", "arbitrary"), name=name)(x, *([w] * n_w), *extra)


X_AHEAD = 3


def _mm_grouped(x, w, starts, counts, slab0, *, col_offs, n_tiles, tn, epi, out_dtype, n_out_cols, name):
    n_rows, k = x.shape
    n_w = len(col_offs)
    ts = MOE_TILE

    def kern(st_ref, ct_ref, s0_ref, x_hbm, *rest):
        w_refs = rest[:n_w]
        o_hbm = rest[n_w]
        xbuf, obuf, wb, pend, sin, sout = rest[n_w + 1:]
        n, e = pl.program_id(0), pl.program_id(1)
        base, cnt = st_ref[e], ct_ref[e]

        def x_copy_from(first_row, r, slot):
            row0 = pl.multiple_of(first_row + r * ts, ts)
            return pltpu.make_async_copy(x_hbm.at[pl.ds(row0, ts), :], xbuf.at[slot], sin.at[slot])

        def x_copy(r, slot):
            return x_copy_from(base, r, slot)

        def x_start_head(first_row, n_tiles_here, cond):
            for r0 in range(X_AHEAD):
                @pl.when(jnp.logical_and(cond, r0 < n_tiles_here))
                def _():
                    x_copy_from(first_row, r0, r0).start()

        def o_copy(r, slot):
            row0 = pl.multiple_of(base + r * ts, ts)
            col0 = pl.multiple_of(n * tn, tn)
            return pltpu.make_async_copy(obuf.at[slot], o_hbm.at[pl.ds(row0, ts), pl.ds(col0, tn)], sout.at[slot])

        def o_drain(slot):
            @pl.when(pend[slot] == 1)
            def _():
                o_copy(0, slot).wait()
                pend[slot] = 0

        def o_emit(r, slot, tile):
            o_drain(slot)
            obuf[slot] = tile
            o_copy(r, slot).start()
            pend[slot] = 1

        first_step = jnp.logical_and(n == 0, e == 0)
        last_step = jnp.logical_and(n == pl.num_programs(0) - 1, e == pl.num_programs(1) - 1)

        @pl.when(first_step)
        def _():
            pend[0] = 0
            pend[1] = 0

        x_start_head(base, cnt, first_step)

        for j in range(n_w):
            wb[j] = w_refs[j][...].astype(BF16)

        def body(r, carry):
            slot = r % (X_AHEAD + 1)

            @pl.when(r + X_AHEAD < cnt)
            def _():
                x_copy(r + X_AHEAD, (r + X_AHEAD) % (X_AHEAD + 1)).start()

            x_copy(r, slot).wait()
            xv = xbuf[slot]
            o_emit(r, r % 2, epi([_dot(xv, wb[j]) for j in range(n_w)]).astype(out_dtype))
            return carry

        lax.fori_loop(0, cnt, body, 0)

        e_next = jnp.where(e == pl.num_programs(1) - 1, 0, e + 1)
        x_start_head(st_ref[e_next], ct_ref[e_next], jnp.logical_not(last_step))

        @pl.when(e == pl.num_programs(1) - 1)
        def _():
            def zero_tile(r, carry):
                o_emit(r, r % 2, jnp.zeros((ts, tn), out_dtype))
                return carry

            lax.fori_loop(cnt, (n_rows - base) // ts, zero_tile, 0)

        @pl.when(last_step)
        def _():
            o_drain(0)
            o_drain(1)

    def w_map(off):
        return lambda n, e, st, ct, s0: (s0[0] + e, 0, n + off)

    gs = pltpu.PrefetchScalarGridSpec(
        num_scalar_prefetch=3,
        grid=(n_tiles, N_EXPERTS),
        in_specs=[pl.BlockSpec(memory_space=pl.ANY)] + [pl.BlockSpec((None, k, tn), w_map(off)) for off in col_offs],
        out_specs=pl.BlockSpec(memory_space=pl.ANY),
        scratch_shapes=[pltpu.VMEM((X_AHEAD + 1, ts, k), BF16), pltpu.VMEM((2, ts, tn), out_dtype),
                        pltpu.VMEM((n_w, k, tn), BF16), pltpu.SMEM((2,), I32),
                        pltpu.SemaphoreType.DMA((X_AHEAD + 1,)), pltpu.SemaphoreType.DMA((2,))])
    return pl.pallas_call(kern, grid_spec=gs, out_shape=jax.ShapeDtypeStruct((n_rows, n_out_cols), out_dtype),
                          compiler_params=_cparams("arbitrary", "arbitrary"),
                          name=name)(starts, counts, slab0, x, *([w] * n_w))


def _row_tile(m_rows):
    return _pick(m_rows, (768, 512, 256))


def _proj_plain(x, w, sel, col0, n_cols, name):
    m_rows = x.shape[0]
    tn = _pick(n_cols, (1024, 512, 256))
    assert col0 % tn == 0
    tm = _row_tile(m_rows)
    return _mm(x, w, sel=sel, col_offs=(col0 // tn,), n_tiles=n_cols // tn, tn=tn, tm=tm,
               epi=lambda accs, m: (accs[0],),
               out_shape=[jax.ShapeDtypeStruct((m_rows, n_cols), BF16)],
               out_specs=[pl.BlockSpec((tm, tn), lambda n, m: (m, n))], name=name)[0]


def _proj_conv_in(x, w, sel):
    m_rows = x.shape[0]
    d3 = w.shape[2] // 3
    tn = _pick(d3, (512, 256))
    tm = _row_tile(m_rows)
    nt = d3 // tn
    o = jax.ShapeDtypeStruct((m_rows, d3), BF16)
    spec = pl.BlockSpec((tm, tn), lambda n, m: (m, n))
    return _mm(x, w, sel=sel, col_offs=(0, nt, 2 * nt), n_tiles=nt, tn=tn, tm=tm,
               epi=lambda accs, m: (accs[0], accs[1] * accs[2]),
               out_shape=[o, o], out_specs=[spec, spec], name="conv_in")


def _proj_gu(x, w, sel):
    m_rows = x.shape[0]
    f = w.shape[2] // 2
    tn = _pick(f, (512, 256))
    tm = _row_tile(m_rows)
    nt = f // tn
    return _mm(x, w, sel=sel, col_offs=(0, nt), n_tiles=nt, tn=tn, tm=tm,
               epi=lambda accs, m: (_silu(accs[0]) * accs[1],),
               out_shape=[jax.ShapeDtypeStruct((m_rows, f), BF16)],
               out_specs=[pl.BlockSpec((tm, tn), lambda n, m: (m, n))], name="ffn_gu")[0]


def _moe_gu(x, w, starts, counts, slab0):
    f = w.shape[2] // 2
    tn = _pick(f, (512, 256))
    nt = f // tn
    return _mm_grouped(x, w, starts, counts, slab0, col_offs=(0, nt), n_tiles=nt, tn=tn,
                       epi=lambda accs: _silu(accs[0]) * accs[1], out_dtype=BF16, n_out_cols=f, name="moe_gu")


def _moe_down(x, w, starts, counts, slab0):
    n_cols = w.shape[2]
    tn = 512
    return _mm_grouped(x, w, starts, counts, slab0, col_offs=(0,), n_tiles=n_cols // tn, tn=tn,
                       epi=lambda accs: accs[0], out_dtype=F32, n_out_cols=n_cols, name="moe_down")


def _proj_qk_rope(x, w, sel, col0, n_cols, gain, cos_t, sin_t, scale, name):
    m_rows = x.shape[0]
    tn = _pick(n_cols, (512, 256, 128))
    assert col0 % tn == 0
    tm = _row_tile(m_rows)

    def epi(accs, m, gain_ref, cos_ref, sin_ref):
        lane = lax.broadcasted_iota(I32, (1, HEAD_DIM), 1)
        first = (lane & (HEAD_DIM // 4)) == 0
        cos_v, sin_v, g = cos_ref[...], sin_ref[...], gain_ref[...]
        heads = []
        for hh in range(tn // HEAD_DIM):
            y = _rms(accs[0][:, hh * HEAD_DIM:(hh + 1) * HEAD_DIM], g)
            rot = jnp.where(first, pltpu.roll(y, HEAD_DIM - HEAD_DIM // 4, 1), pltpu.roll(y, HEAD_DIM // 4, 1))
            heads.append((y * cos_v + rot * sin_v) * scale)
        return (heads,)

    return _mm(x, w, sel=sel, col_offs=(col0 // tn,), n_tiles=n_cols // tn, tn=tn, tm=tm, epi=epi,
               extra=(gain.reshape(1, HEAD_DIM), cos_t, sin_t),
               extra_specs=(pl.BlockSpec((1, HEAD_DIM), lambda n, m: (0, 0)),
                            pl.BlockSpec((tm, HEAD_DIM), lambda n, m: (m, 0)),
                            pl.BlockSpec((tm, HEAD_DIM), lambda n, m: (m, 0))),
               out_shape=[jax.ShapeDtypeStruct((m_rows, n_cols), BF16)],
               out_specs=[pl.BlockSpec((tm, tn), lambda n, m: (m, n))], name=name)[0]


def _proj_resid(x, w, sel, h, gate, n_ctx, name):
    m_rows, k = x.shape
    n_cols = w.shape[2]
    big_k = k > 4096
    tn = 512 if big_k else _pick(n_cols, (1024, 512, 256))
    tm = _pick(m_rows, (512, 384, 256)) if big_k else _row_tile(m_rows)

    def epi(accs, m, h_ref, gate_ref):
        row = m * tm + lax.broadcasted_iota(I32, (tm, 1), 0)
        g = jnp.where(row < n_ctx, gate_ref[0:1, :], gate_ref[1:2, :])
        return (h_ref[...] + g * accs[0],)

    return _mm(x, w, sel=sel, col_offs=(0,), n_tiles=n_cols // tn, tn=tn, tm=tm, epi=epi,
               extra=(h, gate),
               extra_specs=(pl.BlockSpec((tm, tn), lambda n, m: (m, n)),
                            pl.BlockSpec((2, tn), lambda n, m: (0, n))),
               out_shape=[jax.ShapeDtypeStruct((m_rows, n_cols), F32)],
               out_specs=[pl.BlockSpec((tm, tn), lambda n, m: (m, n))], name=name)[0]


def _conv_gate_kernel(b_ref, u_ref, up_ref, un_ref, cw_ref, z_ref, *, tm, n_ctx_blocks, n_blocks):
    i = pl.program_id(0)
    u = u_ref[...].astype(F32)
    prev_ok = jnp.logical_and(i != 0, i != n_ctx_blocks).astype(F32)
    next_ok = jnp.logical_and(i != n_ctx_blocks - 1, i != n_blocks - 1).astype(F32)
    prow = up_ref[15:16, :].astype(F32) * prev_ok
    nrow = un_ref[0:1, :].astype(F32) * next_ok
    row = lax.broadcasted_iota(I32, (tm, 1), 0)
    um1 = jnp.where(row == 0, prow, pltpu.roll(u, 1, 0))
    up1 = jnp.where(row == tm - 1, nrow, pltpu.roll(u, tm - 1, 0))
    conv = cw_ref[0:1, :] * um1 + cw_ref[1:2, :] * u + cw_ref[2:3, :] * up1
    z_ref[...] = (b_ref[...].astype(F32) * conv).astype(z_ref.dtype)


def _conv_gate(b, u, conv_w, n_ctx):
    m, d = u.shape
    tm = ROW_BLOCK
    tc = _pick(d, (1024, 512, 256, 128))
    hb = 16
    nb = m // tm
    last_hb = m // hb - 1
    kern = functools.partial(_conv_gate_kernel, tm=tm, n_ctx_blocks=n_ctx // tm, n_blocks=nb)
    return pl.pallas_call(
        kern,
        grid=(nb, d // tc),
        in_specs=[pl.BlockSpec((tm, tc), lambda i, j: (i, j)),
                  pl.BlockSpec((tm, tc), lambda i, j: (i, j)),
                  pl.BlockSpec((hb, tc), lambda i, j: (jnp.maximum(i * (tm // hb) - 1, 0), j)),
                  pl.BlockSpec((hb, tc), lambda i, j: (jnp.minimum((i + 1) * (tm // hb), last_hb), j)),
                  pl.BlockSpec((3, tc), lambda i, j: (0, j))],
        out_specs=pl.BlockSpec((tm, tc), lambda i, j: (i, j)),
        out_shape=jax.ShapeDtypeStruct((m, d), BF16),
        compiler_params=_cparams("arbitrary", "arbitrary"),
        name="conv_gate",
    )(b, u, u, u, conv_w)


def _na_bias_tables(rpb, rows):
    nb = rows // NA_QROWS
    n_h = rpb.shape[0]
    c = np.arange(GRID_W)[:, None]
    kc = np.arange(GRID_W)[None, :]
    c0 = np.clip(c - NA_KW // 2, 0, GRID_W - NA_KW)
    c_valid = (kc >= c0) & (kc < c0 + NA_KW)
    c_sel = (kc - c + NA_KW - 1)[:, :, None] == np.arange(2 * NA_KW - 1)[None, None, :]
    c_sel = (c_sel & c_valid[:, :, None]).astype(np.float32)
    r_sel, r_valid = [], []
    for b in (0, 1, nb - 1):
        w0 = NA_QROWS * int(np.clip(b - 1, 0, nb - 3))
        ar = NA_QROWS * b + np.arange(NA_QROWS)[:, None]
        kr = w0 + np.arange(NA_WROWS)[None, :]
        r0 = np.clip(ar - NA_KH // 2, 0, rows - NA_KH)
        ok = (kr >= r0) & (kr < r0 + NA_KH)
        sel = (kr - ar + NA_KH - 1)[:, :, None] == np.arange(2 * NA_KH - 1)[None, None, :]
        r_sel.append((sel & ok[:, :, None]).astype(np.float32))
        r_valid.append(ok)
    r_sel, r_valid = np.stack(r_sel), np.stack(r_valid)
    t = jnp.einsum("pijr,hrd,ckd->phicjk", jnp.asarray(r_sel), rpb, jnp.asarray(c_sel),
                   precision=lax.Precision.HIGHEST)
    valid = r_valid[:, None, :, None, :, None] & c_valid[None, None, None, :, None, :]
    t = jnp.where(valid, t, MASKED).reshape(3, n_h, NA_QROWS * GRID_W, NA_WROWS * GRID_W)
    return jnp.concatenate([jnp.full((1,) + t.shape[1:], MASKED, F32), t], axis=0)


def _na_kernel(q_ref, k0_ref, k1_ref, k2_ref, kc_ref, v0_ref, v1_ref, v2_ref, vc_ref, bias_ref, o_ref, *, scale):
    blk = ROW_BLOCK
    for h in range(NA_HEADS):
        hs = slice(h * HEAD_DIM, (h + 1) * HEAD_DIM)
        q = q_ref[:, hs]
        ss = [_dot_nt(q, kr[:, hs]) * scale + bias_ref[h, :, j * blk:(j + 1) * blk]
              for j, kr in enumerate((k0_ref, k1_ref, k2_ref))]
        ss.append(_dot_nt(q, kc_ref[:, hs]) * scale)
        mx = functools.reduce(jnp.maximum, [jnp.max(s, axis=-1, keepdims=True) for s in ss])
        ps = [jnp.exp(s - mx) for s in ss]
        den = functools.reduce(jnp.add, [jnp.sum(p, axis=-1, keepdims=True) for p in ps])
        acc = functools.reduce(jnp.add, [_dot(p.astype(BF16), vr[:, hs])
                                         for p, vr in zip(ps, (v0_ref, v1_ref, v2_ref, vc_ref))])
        o_ref[:, hs] = (acc / den).astype(o_ref.dtype)


def _na_attention(qkv, bias, n_ctx):
    m, d3 = qkv.shape
    d = d3 // 3
    blk = ROW_BLOCK
    assert n_ctx == blk and NA_QROWS * GRID_W == blk
    nq = m // blk
    nb = nq - 1

    def kv_map(j, col):
        return lambda g: (1 + jnp.clip(g - 2, 0, nb - 3) + j, col)

    def bias_map(g):
        return (jnp.where(g == 0, 0, jnp.where(g == 1, 1, jnp.where(g == nq - 1, 3, 2))), 0, 0, 0)

    blkspec = lambda imap: pl.BlockSpec((blk, d), imap)
    return pl.pallas_call(
        functools.partial(_na_kernel, scale=HEAD_DIM ** -0.5),
        grid=(nq,),
        in_specs=[blkspec(lambda g: (g, 0)),
                  blkspec(kv_map(0, 1)), blkspec(kv_map(1, 1)), blkspec(kv_map(2, 1)), blkspec(lambda g: (0, 1)),
                  blkspec(kv_map(0, 2)), blkspec(kv_map(1, 2)), blkspec(kv_map(2, 2)), blkspec(lambda g: (0, 2)),
                  pl.BlockSpec((None, NA_HEADS, blk, 3 * blk), bias_map, pipeline_mode=pl.Buffered(1))],
        out_specs=blkspec(lambda g: (g, 0)),
        out_shape=jax.ShapeDtypeStruct((m, d), BF16),
        compiler_params=_cparams("arbitrary"),
        name="na_attention",
    )(qkv, qkv, qkv, qkv, qkv, qkv, qkv, qkv, qkv, bias)


def _rope_tables(n_ctx, n_lat):
    t = jnp.arange(n_lat)
    row = (t // GRID_W).astype(F32)
    col = (t % GRID_W).astype(F32)
    quarter = HEAD_DIM // 4
    inv_freq = ROPE_THETA ** (-jnp.arange(quarter, dtype=F32) / quarter)
    ar, ac = row[:, None] * inv_freq, col[:, None] * inv_freq
    cos_t = jnp.concatenate([jnp.cos(ar), jnp.cos(ar), jnp.cos(ac), jnp.cos(ac)], axis=1)
    sin_t = jnp.concatenate([-jnp.sin(ar), jnp.sin(ar), -jnp.sin(ac), jnp.sin(ac)], axis=1)
    cos_t = jnp.concatenate([jnp.ones((n_ctx, HEAD_DIM), F32), cos_t], axis=0)
    sin_t = jnp.concatenate([jnp.zeros((n_ctx, HEAD_DIM), F32), sin_t], axis=0)
    return cos_t, sin_t


GQA_SUBCHUNKS = 32


def _gqa_kernel(q_ref, k_ref, v_ref, o_ref, *, n_ctx, tk, n_lat_chunks, group):
    is_ctx = pl.program_id(1) * ROW_BLOCK < n_ctx
    n_it = jnp.where(is_ctx, 0, n_lat_chunks)
    qs = [q_ref[:, g * HEAD_DIM:(g + 1) * HEAD_DIM] for g in range(group)]
    kc, vc = k_ref[0:n_ctx, :], v_ref[0:n_ctx, :]
    state = []
    for g in range(group):
        s = _dot_nt(qs[g], kc)
        mx = jnp.max(s, axis=-1, keepdims=True)
        state += [mx, _dot(jnp.exp2(s - mx).astype(BF16), vc)]

    def body(c, carry):
        out = list(carry)
        for sub in range(GQA_SUBCHUNKS):
            off = pl.multiple_of(n_ctx + c * tk + sub * (tk // GQA_SUBCHUNKS), ROW_BLOCK)
            kk = k_ref[pl.ds(off, tk // GQA_SUBCHUNKS), :]
            vv = v_ref[pl.ds(off, tk // GQA_SUBCHUNKS), :]
            for g in range(group):
                mx, acc = out[2 * g:2 * g + 2]
                s = _dot_nt(qs[g], kk)
                mx_new = jnp.maximum(mx, jnp.max(s, axis=-1, keepdims=True))
                out[2 * g] = mx_new
                out[2 * g + 1] = jnp.exp2(mx - mx_new) * acc + _dot(jnp.exp2(s - mx_new).astype(BF16), vv)
        return tuple(out)

    state = lax.fori_loop(0, n_it, body, tuple(state))
    for g in range(group):
        acc = state[2 * g + 1]
        o_ref[:, g * HEAD_DIM:(g + 1) * HEAD_DIM] = (acc[:, :HEAD_DIM] / acc[:, HEAD_DIM:]).astype(o_ref.dtype)


def _gqa_attention(q, k, v, n_ctx):
    m, dq = q.shape
    group = GQA_HEADS // GQA_KV_HEADS
    gw = group * HEAD_DIM
    tq = ROW_BLOCK
    n_lat = m - n_ctx
    tk = _pick(n_lat, (8192, 4096, 2048, 1024, 512, 256))
    kern = functools.partial(_gqa_kernel, n_ctx=n_ctx, tk=tk, n_lat_chunks=n_lat // tk, group=group)
    v = jnp.concatenate([v.reshape(m, GQA_KV_HEADS, HEAD_DIM), jnp.ones((m, GQA_KV_HEADS, HEAD_DIM), v.dtype)],
                        axis=2).reshape(m, 2 * GQA_KV_HEADS * HEAD_DIM)
    return pl.pallas_call(
        kern,
        grid=(GQA_KV_HEADS, m // tq),
        in_specs=[pl.BlockSpec((tq, gw), lambda kh, i: (i, kh)),
                  pl.BlockSpec((m, HEAD_DIM), lambda kh, i: (0, kh)),
                  pl.BlockSpec((m, 2 * HEAD_DIM), lambda kh, i: (0, kh))],
        out_specs=pl.BlockSpec((tq, gw), lambda kh, i: (i, kh)),
        out_shape=jax.ShapeDtypeStruct((m, dq), BF16),
        compiler_params=_cparams("arbitrary", "arbitrary"),
        name="gqa_attention",
    )(q, k, v)


MOE_TILE = 256


def _moe_plan(ids):
    m = ids.shape[0]
    tm = MOE_TILE
    n_rows = -(-(TOP_K * m + N_EXPERTS * (tm - 1)) // tm) * tm
    flat_e = ids.reshape(-1)
    onehot = (flat_e[:, None] == jnp.arange(N_EXPERTS, dtype=I32)[None, :]).astype(I32)
    csum = jnp.cumsum(onehot, axis=0)
    rank = jnp.sum(csum * onehot, axis=1) - 1
    tiles = (csum[-1] + tm - 1) // tm
    starts = (jnp.cumsum(tiles) - tiles) * tm
    pos = jnp.sum(starts[None, :] * onehot, axis=1) + rank
    tok = jnp.arange(TOP_K * m, dtype=I32) // TOP_K
    src = jnp.zeros((n_rows,), I32).at[pos].set(tok)
    return src, pos.astype(I32), starts.astype(I32), tiles.astype(I32)


def _prefetch_rows(idx_ref, src_hbm, buf, sem, i, n_steps, n_rows):
    def row_copy(slot, r, row):
        return pltpu.make_async_copy(src_hbm.at[pl.ds(row, 1), :], buf.at[slot, pl.ds(r, 1), :], sem.at[slot])

    def start_all(step):
        slot = step % 2
        for r in range(n_rows):
            row_copy(slot, r, idx_ref[step * n_rows + r]).start(priority=r % 2)

    @pl.when(i == 0)
    def _():
        start_all(i)

    @pl.when(i + 1 < n_steps)
    def _():
        start_all(i + 1)

    for r in range(n_rows):
        row_copy(i % 2, r, 0).wait()


def _gather_norm_kernel(src_ref, tok_ref, h_hbm, g_ref, ssc_ref, ssl_ref, o_ref, buf, sem, *, tg, n_ctx):
    i = pl.program_id(0)
    _prefetch_rows(src_ref, h_hbm, buf, sem, i, pl.num_programs(0), tg)
    y = _rms(buf[i % 2], g_ref[...])
    scale, shift = ssl_ref[1:2, :], ssl_ref[0:1, :]
    if n_ctx > 0:
        is_ctx = tok_ref[...] < n_ctx
        scale = jnp.where(is_ctx, ssc_ref[1:2, :], scale)
        shift = jnp.where(is_ctx, ssc_ref[0:1, :], shift)
    o_ref[...] = (y * (1.0 + scale) + shift).astype(o_ref.dtype)


def _gather_norm(h, src, gain, ss, n_ctx):
    m, d = h.shape
    n_rows = src.shape[0]
    tg = ROW_BLOCK
    kern = functools.partial(_gather_norm_kernel, tg=tg, n_ctx=n_ctx)
    gs = pltpu.PrefetchScalarGridSpec(
        num_scalar_prefetch=1,
        grid=(n_rows // tg,),
        in_specs=[pl.BlockSpec((tg, 1), lambda i, *pf: (i, 0)),
                  pl.BlockSpec(memory_space=pl.ANY),
                  pl.BlockSpec((1, d), lambda i, *pf: (0, 0)),
                  pl.BlockSpec((None, 2, d), lambda i, *pf: (0, 0, 0)),
                  pl.BlockSpec((None, 2, d), lambda i, *pf: (1, 0, 0))],
        out_specs=pl.BlockSpec((tg, d), lambda i, *pf: (i, 0)),
        scratch_shapes=[pltpu.VMEM((2, tg, d), F32), pltpu.SemaphoreType.DMA((2,))])
    return pl.pallas_call(kern, grid_spec=gs, out_shape=jax.ShapeDtypeStruct((n_rows, d), BF16),
                          compiler_params=_cparams("arbitrary"),
                          name="moe_gather")(src, src.reshape(n_rows, 1), h, gain.reshape(1, d), ss, ss)


def _combine_kernel(pos_ref, h_ref, gate_ref, wts_ref, y_hbm, *rest, tg, n_ctx, post):
    n_out = _N_COMBINE_OUTS[post]
    buf, sem = rest[-2:]
    outs = rest[-2 - n_out:-2]
    post_refs = rest[:-2 - n_out]
    i = pl.program_id(0)
    _prefetch_rows(pos_ref, y_hbm, buf, sem, i, pl.num_programs(0), tg * TOP_K)
    row = i * tg + lax.broadcasted_iota(I32, (tg, 1), 0)
    g = jnp.where(row < n_ctx, gate_ref[0:1, :], gate_ref[1:2, :])
    slot = i % 2
    mix = wts_ref[:, 0:1] * buf[slot, 0:tg, :] + wts_ref[:, 1:2] * buf[slot, tg:TOP_K * tg, :]
    h_new = h_ref[...] + g * mix
    if post == "final":
        outs[0][...] = _rms(h_new, post_refs[0][...])
        return
    outs[0][...] = h_new
    if post == "mod":
        ss_ref = post_refs[1]
        outs[1][...] = (_rms(h_new, post_refs[0][...]) * (1.0 + ss_ref[1:2, :]) + ss_ref[0:1, :]).astype(BF16)


_N_COMBINE_OUTS = {None: 1, "final": 1, "mod": 2}


def _combine(h, y, pos, wts, gate, n_ctx, post=None, post_gain=None, post_ss=None):
    m, d = h.shape
    tg = ROW_BLOCK
    pos_tiles = pos.reshape(m // tg, tg, TOP_K).transpose(0, 2, 1).reshape(-1)
    kern = functools.partial(_combine_kernel, tg=tg, n_ctx=n_ctx, post=post)
    row_spec = pl.BlockSpec((tg, d), lambda i, *pf: (i, 0))
    in_specs = [row_spec,
                pl.BlockSpec((2, d), lambda i, *pf: (0, 0)),
                pl.BlockSpec((tg, LANES), lambda i, *pf: (i, 0)),
                pl.BlockSpec(memory_space=pl.ANY)]
    args = [pos_tiles, h, gate, wts, y]
    out_shape = [jax.ShapeDtypeStruct((m, d), F32)]
    if post is not None:
        in_specs.append(pl.BlockSpec((1, d), lambda i, *pf: (0, 0)))
        args.append(post_gain.reshape(1, d))
    if post == "mod":
        seg = _seg_map(n_ctx // tg)
        in_specs.append(pl.BlockSpec((None, 2, d), lambda i, *pf: seg(i)))
        args.append(post_ss)
        out_shape.append(jax.ShapeDtypeStruct((m, d), BF16))
    gs = pltpu.PrefetchScalarGridSpec(
        num_scalar_prefetch=1,
        grid=(m // tg,),
        in_specs=in_specs,
        out_specs=[row_spec] * len(out_shape),
        scratch_shapes=[pltpu.VMEM((2, TOP_K * tg, d), F32), pltpu.SemaphoreType.DMA((2,))])
    res = pl.pallas_call(kern, grid_spec=gs, out_shape=out_shape,
                         compiler_params=_cparams("arbitrary"), name="moe_combine")(*args)
    return res if post == "mod" else res[0]


def _moe(h, gain, ss, gate, w_router, w_gu, w_down, j, n_ctx, **post):
    ids, wts = _router(h, gain, ss, w_router[j], n_ctx)
    src, pos, starts, tiles = _moe_plan(ids[:, :TOP_K])
    xs = _gather_norm(h, src, gain, ss, n_ctx)
    slab0 = jnp.full((1,), j * N_EXPERTS, I32)
    act = _moe_gu(xs, w_gu.reshape((-1,) + w_gu.shape[2:]), starts, tiles, slab0)
    y = _moe_down(act, w_down.reshape((-1,) + w_down.shape[2:]), starts, tiles, slab0)
    return _combine(h, y, pos, wts, gate, n_ctx, **post)


def kernel(x, c, ctx, c_ctx, ada_w, ada_b, norm_mix, norm_ffn, norm_final, conv_w_in, conv_w, conv_w_out,
           na_w_qkv, na_rpb, na_w_out, gqa_w_q, gqa_w_kv, gqa_q_norm, gqa_k_norm, gqa_w_out, ffn_w_gu,
           ffn_w_down, moe_w_router, moe_w_gu, moe_w_down):
    bsz, n_lat, d = x.shape
    n_ctx_full = ctx.shape[1]
    depth = ada_w.shape[0]
    assert bsz == 1 and n_ctx_full == ROW_BLOCK and n_lat % (GRID_W * NA_QROWS) == 0
    assert n_lat // GRID_W >= NA_WROWS

    cvec = jnp.zeros((8, d), F32).at[0].set(c_ctx).at[1].set(c[0])
    mod = _ada(cvec, ada_w, ada_b)[:, :2].reshape(depth, 2, 6, d)

    h = jnp.concatenate([ctx[0], x[0]], axis=0)
    n_ctx = n_ctx_full
    cos_t, sin_t = _rope_tables(n_ctx_full, n_lat)

    a_next = None
    for i in range(depth):
        if i == depth - 1:
            h = h[n_ctx:]
            n_ctx = 0
        kind, j = i % N_MIXERS, i // N_MIXERS
        a = a_next if a_next is not None else _norm_mod(h, norm_mix[i], mod[i, :, 0:2], n_ctx)
        a_next = None
        gate1 = mod[i, :, 2]
        if kind == 0:
            b_gate, u = _proj_conv_in(a, conv_w_in, j)
            z = _conv_gate(b_gate, u, conv_w[j], n_ctx)
            h = _proj_resid(z, conv_w_out, j, h, gate1, n_ctx, "conv_out")
        elif kind == 1:
            qkv = _proj_plain(a, na_w_qkv, j, 0, 3 * d, "na_qkv")
            bias = _na_bias_tables(na_rpb[j], n_lat // GRID_W)
            o = _na_attention(qkv, bias, n_ctx)
            h = _proj_resid(o, na_w_out, j, h, gate1, n_ctx, "na_out")
        else:
            dq = GQA_HEADS * HEAD_DIM
            dkv = GQA_KV_HEADS * HEAD_DIM
            q = _proj_qk_rope(a, gqa_w_q, j, 0, dq, gqa_q_norm[j], cos_t, sin_t, HEAD_DIM ** -0.5 * LOG2_E, "gqa_q")
            k = _proj_qk_rope(a, gqa_w_kv, j, 0, dkv, gqa_k_norm[j], cos_t, sin_t, 1.0, "gqa_k")
            v = _proj_plain(a, gqa_w_kv, j, dkv, dkv, "gqa_v")
            o = _gqa_attention(q, k, v, n_ctx)
            h = _proj_resid(o, gqa_w_out, j, h, gate1, n_ctx, "gqa_out")
        gate2 = mod[i, :, 5]
        if i % 2 == 0:
            f = _norm_mod(h, norm_ffn[i], mod[i, :, 3:5], n_ctx)
            act = _proj_gu(f, ffn_w_gu, i // 2)
            h = _proj_resid(act, ffn_w_down, i // 2, h, gate2, n_ctx, "ffn_down")
        else:
            moe_args = (h, norm_ffn[i], mod[i, :, 3:5], gate2, moe_w_router, moe_w_gu, moe_w_down, i // 2, n_ctx)
            if i == depth - 1:
                return _moe(*moe_args, post="final", post_gain=norm_final)[None]
            if i + 1 < depth - 1:
                h, a_next = _moe(*moe_args, post="mod", post_gain=norm_mix[i + 1], post_ss=mod[i + 1, :, 0:2])
            else:
                h = _moe(*moe_args)
    return _final_norm(h, norm_final)[None]
```

```python
import functools

import numpy as np
import jax
import jax.numpy as jnp
from jax import lax
from jax.experimental import pallas as pl
from jax.experimental.pallas import tpu as pltpu

F32 = jnp.float32
BF16 = jnp.bfloat16
I32 = jnp.int32

EPS = 1e-6
GRID_W = 64
N_MIXERS = 3
NA_HEADS = 16
NA_KH = 8
NA_KW = 16
NA_QROWS = 4
NA_WROWS = NA_QROWS + NA_KH
GQA_HEADS = 16
GQA_KV_HEADS = 4
HEAD_DIM = 128
ROPE_THETA = 10000.0
N_EXPERTS = 8
TOP_K = 2
LANES = 128
ROW_BLOCK = 256
MASKED = -1e30
LOG2_E = float(np.log2(np.e))
VMEM_LIMIT = 60 * 1024 * 1024


def _cparams(*sem):
    return pltpu.CompilerParams(dimension_semantics=sem, vmem_limit_bytes=VMEM_LIMIT)


def _pick(n, cands):
    for c in cands:
        if n % c == 0:
            return c
    raise ValueError(f"no tile for {n} in {cands}")


def _dot(a, b):
    return jnp.dot(a, b, preferred_element_type=F32)


def _dot_nt(a, b):
    return lax.dot_general(a, b, (((1,), (1,)), ((), ())), preferred_element_type=F32)


def _silu(x):
    return x * (1.0 / (1.0 + jnp.exp(-x)))


def _rms(x, gain):
    return x * lax.rsqrt(jnp.mean(x * x, axis=-1, keepdims=True) + EPS) * gain


def _ada_kernel(ct_ref, w_ref, b_ref, o_ref, s_scr):
    @pl.when(jnp.logical_and(pl.program_id(0) == 0, pl.program_id(1) == 0))
    def _():
        st = _silu(ct_ref[...])
        for r in range(2):
            s_scr[r] = jnp.broadcast_to(st[:, r:r + 1], s_scr.shape[1:])

    tn = o_ref.shape[1]
    row = lax.broadcasted_iota(I32, (8, LANES), 0)
    for cb in range(tn // LANES):
        w = w_ref[:, cb * LANES:(cb + 1) * LANES]
        sums = [jnp.sum(w * s_scr[r], axis=0, keepdims=True) for r in range(2)]
        tile = jnp.where(row == 0, sums[0], jnp.where(row == 1, sums[1], 0.0))
        o_ref[:, cb * LANES:(cb + 1) * LANES] = tile + b_ref[:, cb * LANES:(cb + 1) * LANES]


def _ada(cvec_t, ada_w, ada_b):
    depth, d, n = ada_w.shape
    tn = _pick(n, (1024, 512, 256, 128))
    return pl.pallas_call(
        _ada_kernel,
        grid=(depth, n // tn),
        in_specs=[pl.BlockSpec((d, 8), lambda i, j: (0, 0)),
                  pl.BlockSpec((None, d, tn), lambda i, j: (i, 0, j)),
                  pl.BlockSpec((None, 1, tn), lambda i, j: (i, 0, j))],
        out_specs=pl.BlockSpec((None, 8, tn), lambda i, j: (i, 0, j)),
        out_shape=jax.ShapeDtypeStruct((depth, 8, n), F32),
        scratch_shapes=[pltpu.VMEM((2, d, LANES), F32)],
        compiler_params=_cparams("arbitrary", "arbitrary"),
        name="ada",
    )(cvec_t, ada_w, ada_b.reshape(depth, 1, n))


def _norm_mod_kernel(h_ref, g_ref, ss_ref, a_ref):
    y = _rms(h_ref[...], g_ref[...])
    a_ref[...] = (y * (1.0 + ss_ref[1:2, :]) + ss_ref[0:1, :]).astype(a_ref.dtype)


def _seg_map(n_ctx_blocks):
    return lambda i: (jnp.where(i < n_ctx_blocks, 0, 1), 0, 0)


def _norm_mod(h, gain, ss, n_ctx):
    m, d = h.shape
    tm = ROW_BLOCK
    return pl.pallas_call(
        _norm_mod_kernel,
        grid=(m // tm,),
        in_specs=[pl.BlockSpec((tm, d), lambda i: (i, 0)),
                  pl.BlockSpec((1, d), lambda i: (0, 0)),
                  pl.BlockSpec((None, 2, d), _seg_map(n_ctx // tm))],
        out_specs=pl.BlockSpec((tm, d), lambda i: (i, 0)),
        out_shape=jax.ShapeDtypeStruct((m, d), BF16),
        compiler_params=_cparams("arbitrary"),
        name="norm_mod",
    )(h, gain.reshape(1, d), ss)


def _router_kernel(h_ref, g_ref, ss_ref, wr_ref, ids_ref, wts_ref):
    y = _rms(h_ref[...], g_ref[...])
    f = y * (1.0 + ss_ref[1:2, :]) + ss_ref[0:1, :]
    logits = jnp.dot(f, wr_ref[...], precision=lax.Precision.HIGHEST, preferred_element_type=F32)
    lane = lax.broadcasted_iota(I32, logits.shape, 1)
    neg = jnp.float32(-jnp.inf)
    l1 = jnp.where(lane < N_EXPERTS, logits, neg)
    m1 = jnp.max(l1, axis=-1, keepdims=True)
    i1 = jnp.min(jnp.where(l1 == m1, lane, LANES), axis=-1, keepdims=True)
    l2 = jnp.where(lane == i1, neg, l1)
    m2 = jnp.max(l2, axis=-1, keepdims=True)
    i2 = jnp.min(jnp.where(l2 == m2, lane, LANES), axis=-1, keepdims=True)
    e2 = jnp.exp(m2 - m1)
    w1 = 1.0 / (1.0 + e2)
    w2 = e2 / (1.0 + e2)
    ids_ref[...] = jnp.where(lane == 0, i1, jnp.where(lane == 1, i2, 0))
    wts_ref[...] = jnp.where(lane == 0, w1, jnp.where(lane == 1, w2, 0.0))


def _router(h, gain, ss, w_router, n_ctx):
    m, d = h.shape
    tm = ROW_BLOCK
    wr = jnp.zeros((d, LANES), F32).at[:, :N_EXPERTS].set(w_router)
    return pl.pallas_call(
        _router_kernel,
        grid=(m // tm,),
        in_specs=[pl.BlockSpec((tm, d), lambda i: (i, 0)),
                  pl.BlockSpec((1, d), lambda i: (0, 0)),
                  pl.BlockSpec((None, 2, d), _seg_map(n_ctx // tm)),
                  pl.BlockSpec((d, LANES), lambda i: (0, 0))],
        out_specs=[pl.BlockSpec((tm, LANES), lambda i: (i, 0)),
                   pl.BlockSpec((tm, LANES), lambda i: (i, 0))],
        out_shape=[jax.ShapeDtypeStruct((m, LANES), I32),
                   jax.ShapeDtypeStruct((m, LANES), F32)],
        compiler_params=_cparams("arbitrary"),
        name="router",
    )(h, gain.reshape(1, d), ss, wr)


def _final_norm_kernel(h_ref, g_ref, o_ref):
    o_ref[...] = _rms(h_ref[...], g_ref[...])


def _final_norm(h, gain):
    m, d = h.shape
    tm = ROW_BLOCK
    return pl.pallas_call(
        _final_norm_kernel,
        grid=(m // tm,),
        in_specs=[pl.BlockSpec((tm, d), lambda i: (i, 0)),
                  pl.BlockSpec((1, d), lambda i: (0, 0))],
        out_specs=pl.BlockSpec((tm, d), lambda i: (i, 0)),
        out_shape=jax.ShapeDtypeStruct((m, d), F32),
        compiler_params=_cparams("arbitrary"),
        name="final_norm",
    )(h, gain.reshape(1, d))


def _mm(x, w, *, sel, col_offs, n_tiles, tn, tm, epi, extra=(), extra_specs=(), out_shape, out_specs, name):
    m_rows, k = x.shape
    n_w = len(col_offs)
    n_ex = len(extra)
    n_out = len(out_shape)

    def kern(*refs):
        x_ref = refs[0]
        w_refs = refs[1:1 + n_w]
        ex = refs[1 + n_w:1 + n_w + n_ex]
        outs = refs[1 + n_w + n_ex:1 + n_w + n_ex + n_out]
        wb = refs[-1]
        m = pl.program_id(1)

        @pl.when(m == 0)
        def _():
            for j in range(n_w):
                wb[j] = w_refs[j][...].astype(BF16)

        xv = x_ref[...]
        accs = [_dot(xv, wb[j]) for j in range(n_w)]
        for o, r in zip(outs, epi(accs, m, *ex)):
            if isinstance(r, list):
                c0 = 0
                for piece in r:
                    o[:, c0:c0 + piece.shape[1]] = piece.astype(o.dtype)
                    c0 += piece.shape[1]
            else:
                o[...] = r.astype(o.dtype)

    def w_map(off):
        return lambda n, m: (sel, 0, n + off)

    in_specs = ([pl.BlockSpec((tm, k), lambda n, m: (m, 0))]
                + [pl.BlockSpec((None, k, tn), w_map(off)) for off in col_offs]
                + list(extra_specs))
    return pl.pallas_call(kern, grid=(n_tiles, m_rows // tm), in_specs=in_specs, out_specs=out_specs,
                          out_shape=out_shape, scratch_shapes=[pltpu.VMEM((n_w, k, tn), BF16)],
                          compiler_params=_cparams("arbitrary", "arbitrary"), name=name)(x, *([w] * n_w), *extra)


X_AHEAD = 3


def _mm_grouped(x, w, starts, counts, slab0, *, col_offs, n_tiles, tn, epi, out_dtype, n_out_cols, name):
    n_rows, k = x.shape
    n_w = len(col_offs)
    ts = MOE_TILE

    def kern(st_ref, ct_ref, s0_ref, x_hbm, *rest):
        w_refs = rest[:n_w]
        o_hbm = rest[n_w]
        xbuf, obuf, wb, pend, sin, sout = rest[n_w + 1:]
        n, e = pl.program_id(0), pl.program_id(1)
        base, cnt = st_ref[e], ct_ref[e]

        def x_copy_from(first_row, r, slot):
            row0 = pl.multiple_of(first_row + r * ts, ts)
            return pltpu.make_async_copy(x_hbm.at[pl.ds(row0, ts), :], xbuf.at[slot], sin.at[slot])

        def x_copy(r, slot):
            return x_copy_from(base, r, slot)

        def x_start_head(first_row, n_tiles_here, cond):
            for r0 in range(X_AHEAD):
                @pl.when(jnp.logical_and(cond, r0 < n_tiles_here))
                def _():
                    x_copy_from(first_row, r0, r0).start()

        def o_copy(r, slot):
            row0 = pl.multiple_of(base + r * ts, ts)
            col0 = pl.multiple_of(n * tn, tn)
            return pltpu.make_async_copy(obuf.at[slot], o_hbm.at[pl.ds(row0, ts), pl.ds(col0, tn)], sout.at[slot])

        def o_drain(slot):
            @pl.when(pend[slot] == 1)
            def _():
                o_copy(0, slot).wait()
                pend[slot] = 0

        def o_emit(r, slot, tile):
            o_drain(slot)
            obuf[slot] = tile
            o_copy(r, slot).start()
            pend[slot] = 1

        first_step = jnp.logical_and(n == 0, e == 0)
        last_step = jnp.logical_and(n == pl.num_programs(0) - 1, e == pl.num_programs(1) - 1)

        @pl.when(first_step)
        def _():
            pend[0] = 0
            pend[1] = 0

        x_start_head(base, cnt, first_step)

        for j in range(n_w):
            wb[j] = w_refs[j][...].astype(BF16)

        def body(r, carry):
            slot = r % (X_AHEAD + 1)

            @pl.when(r + X_AHEAD < cnt)
            def _():
                x_copy(r + X_AHEAD, (r + X_AHEAD) % (X_AHEAD + 1)).start()

            x_copy(r, slot).wait()
            xv = xbuf[slot]
            o_emit(r, r % 2, epi([_dot(xv, wb[j]) for j in range(n_w)]).astype(out_dtype))
            return carry

        lax.fori_loop(0, cnt, body, 0)

        e_next = jnp.where(e == pl.num_programs(1) - 1, 0, e + 1)
        x_start_head(st_ref[e_next], ct_ref[e_next], jnp.logical_not(last_step))

        @pl.when(e == pl.num_programs(1) - 1)
        def _():
            def zero_tile(r, carry):
                o_emit(r, r % 2, jnp.zeros((ts, tn), out_dtype))
                return carry

            lax.fori_loop(cnt, (n_rows - base) // ts, zero_tile, 0)

        @pl.when(last_step)
        def _():
            o_drain(0)
            o_drain(1)

    def w_map(off):
        return lambda n, e, st, ct, s0: (s0[0] + e, 0, n + off)

    gs = pltpu.PrefetchScalarGridSpec(
        num_scalar_prefetch=3,
        grid=(n_tiles, N_EXPERTS),
        in_specs=[pl.BlockSpec(memory_space=pl.ANY)] + [pl.BlockSpec((None, k, tn), w_map(off)) for off in col_offs],
        out_specs=pl.BlockSpec(memory_space=pl.ANY),
        scratch_shapes=[pltpu.VMEM((X_AHEAD + 1, ts, k), BF16), pltpu.VMEM((2, ts, tn), out_dtype),
                        pltpu.VMEM((n_w, k, tn), BF16), pltpu.SMEM((2,), I32),
                        pltpu.SemaphoreType.DMA((X_AHEAD + 1,)), pltpu.SemaphoreType.DMA((2,))])
    return pl.pallas_call(kern, grid_spec=gs, out_shape=jax.ShapeDtypeStruct((n_rows, n_out_cols), out_dtype),
                          compiler_params=_cparams("arbitrary", "arbitrary"),
                          name=name)(starts, counts, slab0, x, *([w] * n_w))


def _row_tile(m_rows):
    return _pick(m_rows, (768, 512, 256))


def _proj_plain(x, w, sel, col0, n_cols, name):
    m_rows = x.shape[0]
    tn = _pick(n_cols, (1024, 512, 256))
    assert col0 % tn == 0
    tm = _row_tile(m_rows)
    return _mm(x, w, sel=sel, col_offs=(col0 // tn,), n_tiles=n_cols // tn, tn=tn, tm=tm,
               epi=lambda accs, m: (accs[0],),
               out_shape=[jax.ShapeDtypeStruct((m_rows, n_cols), BF16)],
               out_specs=[pl.BlockSpec((tm, tn), lambda n, m: (m, n))], name=name)[0]


def _proj_conv_in(x, w, sel):
    m_rows = x.shape[0]
    d3 = w.shape[2] // 3
    tn = _pick(d3, (512, 256))
    tm = _row_tile(m_rows)
    nt = d3 // tn
    o = jax.ShapeDtypeStruct((m_rows, d3), BF16)
    spec = pl.BlockSpec((tm, tn), lambda n, m: (m, n))
    return _mm(x, w, sel=sel, col_offs=(0, nt, 2 * nt), n_tiles=nt, tn=tn, tm=tm,
               epi=lambda accs, m: (accs[0], accs[1] * accs[2]),
               out_shape=[o, o], out_specs=[spec, spec], name="conv_in")


def _proj_gu(x, w, sel):
    m_rows = x.shape[0]
    f = w.shape[2] // 2
    tn = _pick(f, (512, 256))
    tm = _row_tile(m_rows)
    nt = f // tn
    return _mm(x, w, sel=sel, col_offs=(0, nt), n_tiles=nt, tn=tn, tm=tm,
               epi=lambda accs, m: (_silu(accs[0]) * accs[1],),
               out_shape=[jax.ShapeDtypeStruct((m_rows, f), BF16)],
               out_specs=[pl.BlockSpec((tm, tn), lambda n, m: (m, n))], name="ffn_gu")[0]


def _moe_gu(x, w, starts, counts, slab0):
    f = w.shape[2] // 2
    tn = _pick(f, (512, 256))
    nt = f // tn
    return _mm_grouped(x, w, starts, counts, slab0, col_offs=(0, nt), n_tiles=nt, tn=tn,
                       epi=lambda accs: _silu(accs[0]) * accs[1], out_dtype=BF16, n_out_cols=f, name="moe_gu")


def _moe_down(x, w, starts, counts, slab0):
    n_cols = w.shape[2]
    tn = 512
    return _mm_grouped(x, w, starts, counts, slab0, col_offs=(0,), n_tiles=n_cols // tn, tn=tn,
                       epi=lambda accs: accs[0], out_dtype=F32, n_out_cols=n_cols, name="moe_down")


def _proj_qk_rope(x, w, sel, col0, n_cols, gain, cos_t, sin_t, scale, name):
    m_rows = x.shape[0]
    tn = _pick(n_cols, (512, 256, 128))
    assert col0 % tn == 0
    tm = _row_tile(m_rows)

    def epi(accs, m, gain_ref, cos_ref, sin_ref):
        src = lax.broadcasted_iota(I32, (HEAD_DIM, HEAD_DIM), 0)
        dst = lax.broadcasted_iota(I32, (HEAD_DIM, HEAD_DIM), 1)
        perm = ((src ^ (HEAD_DIM // 4)) == dst).astype(BF16)
        cos_v, sin_v, g = cos_ref[...], sin_ref[...], gain_ref[...]
        heads = []
        for hh in range(tn // HEAD_DIM):
            y = _rms(accs[0][:, hh * HEAD_DIM:(hh + 1) * HEAD_DIM], g)
            y_hi = y.astype(BF16)
            y_lo = (y - y_hi.astype(F32)).astype(BF16)
            rot = _dot(y_hi, perm) + _dot(y_lo, perm)
            heads.append((y * cos_v + rot * sin_v) * scale)
        return (heads,)

    return _mm(x, w, sel=sel, col_offs=(col0 // tn,), n_tiles=n_cols // tn, tn=tn, tm=tm, epi=epi,
               extra=(gain.reshape(1, HEAD_DIM), cos_t, sin_t),
               extra_specs=(pl.BlockSpec((1, HEAD_DIM), lambda n, m: (0, 0)),
                            pl.BlockSpec((tm, HEAD_DIM), lambda n, m: (m, 0)),
                            pl.BlockSpec((tm, HEAD_DIM), lambda n, m: (m, 0))),
               out_shape=[jax.ShapeDtypeStruct((m_rows, n_cols), BF16)],
               out_specs=[pl.BlockSpec((tm, tn), lambda n, m: (m, n))], name=name)[0]


def _proj_resid(x, w, sel, h, gate, n_ctx, name):
    m_rows, k = x.shape
    n_cols = w.shape[2]
    big_k = k > 4096
    tn = 512 if big_k else _pick(n_cols, (1024, 512, 256))
    tm = _pick(m_rows, (512, 384, 256)) if big_k else _row_tile(m_rows)

    def epi(accs, m, h_ref, gate_ref):
        row = m * tm + lax.broadcasted_iota(I32, (tm, 1), 0)
        g = jnp.where(row < n_ctx, gate_ref[0:1, :], gate_ref[1:2, :])
        return (h_ref[...] + g * accs[0],)

    return _mm(x, w, sel=sel, col_offs=(0,), n_tiles=n_cols // tn, tn=tn, tm=tm, epi=epi,
               extra=(h, gate),
               extra_specs=(pl.BlockSpec((tm, tn), lambda n, m: (m, n)),
                            pl.BlockSpec((2, tn), lambda n, m: (0, n))),
               out_shape=[jax.ShapeDtypeStruct((m_rows, n_cols), F32)],
               out_specs=[pl.BlockSpec((tm, tn), lambda n, m: (m, n))], name=name)[0]


def _conv_gate_kernel(b_ref, u_ref, up_ref, un_ref, cw_ref, z_ref, *, tm, n_ctx_blocks, n_blocks):
    i = pl.program_id(0)
    u = u_ref[...].astype(F32)
    prev_ok = jnp.logical_and(i != 0, i != n_ctx_blocks).astype(F32)
    next_ok = jnp.logical_and(i != n_ctx_blocks - 1, i != n_blocks - 1).astype(F32)
    prow = up_ref[15:16, :].astype(F32) * prev_ok
    nrow = un_ref[0:1, :].astype(F32) * next_ok
    row = lax.broadcasted_iota(I32, (tm, 1), 0)
    um1 = jnp.where(row == 0, prow, pltpu.roll(u, 1, 0))
    up1 = jnp.where(row == tm - 1, nrow, pltpu.roll(u, tm - 1, 0))
    conv = cw_ref[0:1, :] * um1 + cw_ref[1:2, :] * u + cw_ref[2:3, :] * up1
    z_ref[...] = (b_ref[...].astype(F32) * conv).astype(z_ref.dtype)


def _conv_gate(b, u, conv_w, n_ctx):
    m, d = u.shape
    tm = ROW_BLOCK
    tc = _pick(d, (1024, 512, 256, 128))
    hb = 16
    nb = m // tm
    last_hb = m // hb - 1
    kern = functools.partial(_conv_gate_kernel, tm=tm, n_ctx_blocks=n_ctx // tm, n_blocks=nb)
    return pl.pallas_call(
        kern,
        grid=(nb, d // tc),
        in_specs=[pl.BlockSpec((tm, tc), lambda i, j: (i, j)),
                  pl.BlockSpec((tm, tc), lambda i, j: (i, j)),
                  pl.BlockSpec((hb, tc), lambda i, j: (jnp.maximum(i * (tm // hb) - 1, 0), j)),
                  pl.BlockSpec((hb, tc), lambda i, j: (jnp.minimum((i + 1) * (tm // hb), last_hb), j)),
                  pl.BlockSpec((3, tc), lambda i, j: (0, j))],
        out_specs=pl.BlockSpec((tm, tc), lambda i, j: (i, j)),
        out_shape=jax.ShapeDtypeStruct((m, d), BF16),
        compiler_params=_cparams("arbitrary", "arbitrary"),
        name="conv_gate",
    )(b, u, u, u, conv_w)


def _na_bias_tables(rpb, rows):
    nb = rows // NA_QROWS
    n_h = rpb.shape[0]
    c = np.arange(GRID_W)[:, None]
    kc = np.arange(GRID_W)[None, :]
    c0 = np.clip(c - NA_KW // 2, 0, GRID_W - NA_KW)
    c_valid = (kc >= c0) & (kc < c0 + NA_KW)
    c_sel = (kc - c + NA_KW - 1)[:, :, None] == np.arange(2 * NA_KW - 1)[None, None, :]
    c_sel = (c_sel & c_valid[:, :, None]).astype(np.float32)
    r_sel, r_valid = [], []
    for b in (0, 1, nb - 1):
        w0 = NA_QROWS * int(np.clip(b - 1, 0, nb - 3))
        ar = NA_QROWS * b + np.arange(NA_QROWS)[:, None]
        kr = w0 + np.arange(NA_WROWS)[None, :]
        r0 = np.clip(ar - NA_KH // 2, 0, rows - NA_KH)
        ok = (kr >= r0) & (kr < r0 + NA_KH)
        sel = (kr - ar + NA_KH - 1)[:, :, None] == np.arange(2 * NA_KH - 1)[None, None, :]
        r_sel.append((sel & ok[:, :, None]).astype(np.float32))
        r_valid.append(ok)
    r_sel, r_valid = np.stack(r_sel), np.stack(r_valid)
    t = jnp.einsum("pijr,hrd,ckd->phicjk", jnp.asarray(r_sel), rpb, jnp.asarray(c_sel),
                   precision=lax.Precision.HIGHEST)
    valid = r_valid[:, None, :, None, :, None] & c_valid[None, None, None, :, None, :]
    t = jnp.where(valid, t, MASKED).reshape(3, n_h, NA_QROWS * GRID_W, NA_WROWS * GRID_W)
    return jnp.concatenate([jnp.full((1,) + t.shape[1:], MASKED, F32), t], axis=0)


def _na_kernel(q_ref, k0_ref, k1_ref, k2_ref, kc_ref, v0_ref, v1_ref, v2_ref, vc_ref, bias_ref, o_ref, *, scale):
    blk = ROW_BLOCK
    for h in range(NA_HEADS):
        hs = slice(h * HEAD_DIM, (h + 1) * HEAD_DIM)
        q = q_ref[:, hs]
        ss = [_dot_nt(q, kr[:, hs]) * scale + bias_ref[h, :, j * blk:(j + 1) * blk]
              for j, kr in enumerate((k0_ref, k1_ref, k2_ref))]
        ss.append(_dot_nt(q, kc_ref[:, hs]) * scale)
        mx = functools.reduce(jnp.maximum, [jnp.max(s, axis=-1, keepdims=True) for s in ss])
        ps = [jnp.exp(s - mx) for s in ss]
        den = functools.reduce(jnp.add, [jnp.sum(p, axis=-1, keepdims=True) for p in ps])
        acc = functools.reduce(jnp.add, [_dot(p.astype(BF16), vr[:, hs])
                                         for p, vr in zip(ps, (v0_ref, v1_ref, v2_ref, vc_ref))])
        o_ref[:, hs] = (acc / den).astype(o_ref.dtype)


def _na_attention(qkv, bias, n_ctx):
    m, d3 = qkv.shape
    d = d3 // 3
    blk = ROW_BLOCK
    assert n_ctx == blk and NA_QROWS * GRID_W == blk
    nq = m // blk
    nb = nq - 1

    def kv_map(j, col):
        return lambda g: (1 + jnp.clip(g - 2, 0, nb - 3) + j, col)

    def bias_map(g):
        return (jnp.where(g == 0, 0, jnp.where(g == 1, 1, jnp.where(g == nq - 1, 3, 2))), 0, 0, 0)

    blkspec = lambda imap: pl.BlockSpec((blk, d), imap)
    return pl.pallas_call(
        functools.partial(_na_kernel, scale=HEAD_DIM ** -0.5),
        grid=(nq,),
        in_specs=[blkspec(lambda g: (g, 0)),
                  blkspec(kv_map(0, 1)), blkspec(kv_map(1, 1)), blkspec(kv_map(2, 1)), blkspec(lambda g: (0, 1)),
                  blkspec(kv_map(0, 2)), blkspec(kv_map(1, 2)), blkspec(kv_map(2, 2)), blkspec(lambda g: (0, 2)),
                  pl.BlockSpec((None, NA_HEADS, blk, 3 * blk), bias_map, pipeline_mode=pl.Buffered(1))],
        out_specs=blkspec(lambda g: (g, 0)),
        out_shape=jax.ShapeDtypeStruct((m, d), BF16),
        compiler_params=_cparams("arbitrary"),
        name="na_attention",
    )(qkv, qkv, qkv, qkv, qkv, qkv, qkv, qkv, qkv, bias)


def _rope_tables(n_ctx, n_lat):
    t = jnp.arange(n_lat)
    row = (t // GRID_W).astype(F32)
    col = (t % GRID_W).astype(F32)
    quarter = HEAD_DIM // 4
    inv_freq = ROPE_THETA ** (-jnp.arange(quarter, dtype=F32) / quarter)
    ar, ac = row[:, None] * inv_freq, col[:, None] * inv_freq
    cos_t = jnp.concatenate([jnp.cos(ar), jnp.cos(ar), jnp.cos(ac), jnp.cos(ac)], axis=1)
    sin_t = jnp.concatenate([-jnp.sin(ar), jnp.sin(ar), -jnp.sin(ac), jnp.sin(ac)], axis=1)
    cos_t = jnp.concatenate([jnp.ones((n_ctx, HEAD_DIM), F32), cos_t], axis=0)
    sin_t = jnp.concatenate([jnp.zeros((n_ctx, HEAD_DIM), F32), sin_t], axis=0)
    return cos_t, sin_t


def _gqa_kernel(q_ref, k_ref, v_ref, o_ref, *, n_ctx, group):
    def attend(n_keys):
        qs = [q_ref[:, g * HEAD_DIM:(g + 1) * HEAD_DIM] for g in range(group)]
        mx, acc = [None] * group, [None] * group
        for c in range(n_keys // ROW_BLOCK):
            kk = k_ref[c * ROW_BLOCK:(c + 1) * ROW_BLOCK, :]
            vv = v_ref[c * ROW_BLOCK:(c + 1) * ROW_BLOCK, :]
            for g in range(group):
                s = _dot_nt(qs[g], kk)
                row_max = jnp.max(s, axis=-1, keepdims=True)
                if c == 0:
                    mx[g] = row_max
                    acc[g] = _dot(jnp.exp2(s - row_max).astype(BF16), vv)
                else:
                    mx_new = jnp.maximum(mx[g], row_max)
                    acc[g] = jnp.exp2(mx[g] - mx_new) * acc[g] + _dot(jnp.exp2(s - mx_new).astype(BF16), vv)
                    mx[g] = mx_new
        for g in range(group):
            o_ref[:, g * HEAD_DIM:(g + 1) * HEAD_DIM] = (
                acc[g][:, :HEAD_DIM] / acc[g][:, HEAD_DIM:]).astype(o_ref.dtype)

    is_ctx = pl.program_id(1) * ROW_BLOCK < n_ctx

    @pl.when(is_ctx)
    def _():
        attend(n_ctx)

    @pl.when(jnp.logical_not(is_ctx))
    def _():
        attend(k_ref.shape[0])


def _gqa_attention(q, k, v, n_ctx):
    m, dq = q.shape
    group = GQA_HEADS // GQA_KV_HEADS
    gw = group * HEAD_DIM
    tq = ROW_BLOCK
    kern = functools.partial(_gqa_kernel, n_ctx=n_ctx, group=group)
    v = jnp.concatenate([v.reshape(m, GQA_KV_HEADS, HEAD_DIM), jnp.ones((m, GQA_KV_HEADS, HEAD_DIM), v.dtype)],
                        axis=2).reshape(m, 2 * GQA_KV_HEADS * HEAD_DIM)
    return pl.pallas_call(
        kern,
        grid=(GQA_KV_HEADS, m // tq),
        in_specs=[pl.BlockSpec((tq, gw), lambda kh, i: (i, kh)),
                  pl.BlockSpec((m, HEAD_DIM), lambda kh, i: (0, kh)),
                  pl.BlockSpec((m, 2 * HEAD_DIM), lambda kh, i: (0, kh))],
        out_specs=pl.BlockSpec((tq, gw), lambda kh, i: (i, kh)),
        out_shape=jax.ShapeDtypeStruct((m, dq), BF16),
        compiler_params=_cparams("arbitrary", "arbitrary"),
        name="gqa_attention",
    )(q, k, v)


MOE_TILE = 256


def _moe_plan(ids):
    m = ids.shape[0]
    tm = MOE_TILE
    n_rows = -(-(TOP_K * m + N_EXPERTS * (tm - 1)) // tm) * tm
    flat_e = ids.reshape(-1)
    onehot = (flat_e[:, None] == jnp.arange(N_EXPERTS, dtype=I32)[None, :]).astype(I32)
    csum = jnp.cumsum(onehot, axis=0)
    rank = jnp.sum(csum * onehot, axis=1) - 1
    tiles = (csum[-1] + tm - 1) // tm
    starts = (jnp.cumsum(tiles) - tiles) * tm
    pos = jnp.sum(starts[None, :] * onehot, axis=1) + rank
    tok = jnp.arange(TOP_K * m, dtype=I32) // TOP_K
    src = jnp.zeros((n_rows,), I32).at[pos].set(tok)
    return src, pos.astype(I32), starts.astype(I32), tiles.astype(I32)


def _prefetch_rows(idx_ref, src_hbm, buf, sem, i, n_steps, n_rows):
    def row_copy(slot, r, row):
        return pltpu.make_async_copy(src_hbm.at[pl.ds(row, 1), :], buf.at[slot, pl.ds(r, 1), :], sem.at[slot])

    def start_all(step):
        slot = step % 2
        for r in range(n_rows):
            row_copy(slot, r, idx_ref[step * n_rows + r]).start(priority=r % 2)

    @pl.when(i == 0)
    def _():
        start_all(i)

    @pl.when(i + 1 < n_steps)
    def _():
        start_all(i + 1)

    for r in range(n_rows):
        row_copy(i % 2, r, 0).wait()


def _gather_norm_kernel(src_ref, tok_ref, h_hbm, g_ref, ssc_ref, ssl_ref, o_ref, buf, sem, *, tg, n_ctx):
    i = pl.program_id(0)
    _prefetch_rows(src_ref, h_hbm, buf, sem, i, pl.num_programs(0), tg)
    y = _rms(buf[i % 2], g_ref[...])
    scale, shift = ssl_ref[1:2, :], ssl_ref[0:1, :]
    if n_ctx > 0:
        is_ctx = tok_ref[...] < n_ctx
        scale = jnp.where(is_ctx, ssc_ref[1:2, :], scale)
        shift = jnp.where(is_ctx, ssc_ref[0:1, :], shift)
    o_ref[...] = (y * (1.0 + scale) + shift).astype(o_ref.dtype)


def _gather_norm(h, src, gain, ss, n_ctx):
    m, d = h.shape
    n_rows = src.shape[0]
    tg = ROW_BLOCK
    kern = functools.partial(_gather_norm_kernel, tg=tg, n_ctx=n_ctx)
    gs = pltpu.PrefetchScalarGridSpec(
        num_scalar_prefetch=1,
        grid=(n_rows // tg,),
        in_specs=[pl.BlockSpec((tg, 1), lambda i, *pf: (i, 0)),
                  pl.BlockSpec(memory_space=pl.ANY),
                  pl.BlockSpec((1, d), lambda i, *pf: (0, 0)),
                  pl.BlockSpec((None, 2, d), lambda i, *pf: (0, 0, 0)),
                  pl.BlockSpec((None, 2, d), lambda i, *pf: (1, 0, 0))],
        out_specs=pl.BlockSpec((tg, d), lambda i, *pf: (i, 0)),
        scratch_shapes=[pltpu.VMEM((2, tg, d), F32), pltpu.SemaphoreType.DMA((2,))])
    return pl.pallas_call(kern, grid_spec=gs, out_shape=jax.ShapeDtypeStruct((n_rows, d), BF16),
                          compiler_params=_cparams("arbitrary"),
                          name="moe_gather")(src, src.reshape(n_rows, 1), h, gain.reshape(1, d), ss, ss)


def _combine_kernel(pos_ref, h_ref, gate_ref, wts_ref, y_hbm, *rest, tg, n_ctx, post):
    n_out = _N_COMBINE_OUTS[post]
    buf, sem = rest[-2:]
    outs = rest[-2 - n_out:-2]
    post_refs = rest[:-2 - n_out]
    i = pl.program_id(0)
    _prefetch_rows(pos_ref, y_hbm, buf, sem, i, pl.num_programs(0), tg * TOP_K)
    row = i * tg + lax.broadcasted_iota(I32, (tg, 1), 0)
    g = jnp.where(row < n_ctx, gate_ref[0:1, :], gate_ref[1:2, :])
    slot = i % 2
    mix = wts_ref[:, 0:1] * buf[slot, 0:tg, :] + wts_ref[:, 1:2] * buf[slot, tg:TOP_K * tg, :]
    h_new = h_ref[...] + g * mix
    if post == "final":
        outs[0][...] = _rms(h_new, post_refs[0][...])
        return
    outs[0][...] = h_new
    if post == "mod":
        ss_ref = post_refs[1]
        outs[1][...] = (_rms(h_new, post_refs[0][...]) * (1.0 + ss_ref[1:2, :]) + ss_ref[0:1, :]).astype(BF16)


_N_COMBINE_OUTS = {None: 1, "final": 1, "mod": 2}


def _combine(h, y, pos, wts, gate, n_ctx, post=None, post_gain=None, post_ss=None):
    m, d = h.shape
    tg = ROW_BLOCK
    pos_tiles = pos.reshape(m // tg, tg, TOP_K).transpose(0, 2, 1).reshape(-1)
    kern = functools.partial(_combine_kernel, tg=tg, n_ctx=n_ctx, post=post)
    row_spec = pl.BlockSpec((tg, d), lambda i, *pf: (i, 0))
    in_specs = [row_spec,
                pl.BlockSpec((2, d), lambda i, *pf: (0, 0)),
                pl.BlockSpec((tg, LANES), lambda i, *pf: (i, 0)),
                pl.BlockSpec(memory_space=pl.ANY)]
    args = [pos_tiles, h, gate, wts, y]
    out_shape = [jax.ShapeDtypeStruct((m, d), F32)]
    if post is not None:
        in_specs.append(pl.BlockSpec((1, d), lambda i, *pf: (0, 0)))
        args.append(post_gain.reshape(1, d))
    if post == "mod":
        seg = _seg_map(n_ctx // tg)
        in_specs.append(pl.BlockSpec((None, 2, d), lambda i, *pf: seg(i)))
        args.append(post_ss)
        out_shape.append(jax.ShapeDtypeStruct((m, d), BF16))
    gs = pltpu.PrefetchScalarGridSpec(
        num_scalar_prefetch=1,
        grid=(m // tg,),
        in_specs=in_specs,
        out_specs=[row_spec] * len(out_shape),
        scratch_shapes=[pltpu.VMEM((2, TOP_K * tg, d), F32), pltpu.SemaphoreType.DMA((2,))])
    res = pl.pallas_call(kern, grid_spec=gs, out_shape=out_shape,
                         compiler_params=_cparams("arbitrary"), name="moe_combine")(*args)
    return res if post == "mod" else res[0]


def _moe(h, gain, ss, gate, w_router, w_gu, w_down, j, n_ctx, **post):
    ids, wts = _router(h, gain, ss, w_router[j], n_ctx)
    src, pos, starts, tiles = _moe_plan(ids[:, :TOP_K])
    xs = _gather_norm(h, src, gain, ss, n_ctx)
    slab0 = jnp.full((1,), j * N_EXPERTS, I32)
    act = _moe_gu(xs, w_gu.reshape((-1,) + w_gu.shape[2:]), starts, tiles, slab0)
    y = _moe_down(act, w_down.reshape((-1,) + w_down.shape[2:]), starts, tiles, slab0)
    return _combine(h, y, pos, wts, gate, n_ctx, **post)


def kernel(x, c, ctx, c_ctx, ada_w, ada_b, norm_mix, norm_ffn, norm_final, conv_w_in, conv_w, conv_w_out,
           na_w_qkv, na_rpb, na_w_out, gqa_w_q, gqa_w_kv, gqa_q_norm, gqa_k_norm, gqa_w_out, ffn_w_gu,
           ffn_w_down, moe_w_router, moe_w_gu, moe_w_down):
    bsz, n_lat, d = x.shape
    n_ctx_full = ctx.shape[1]
    depth = ada_w.shape[0]
    assert bsz == 1 and n_ctx_full == ROW_BLOCK and n_lat % (GRID_W * NA_QROWS) == 0
    assert n_lat // GRID_W >= NA_WROWS

    cvec_t = jnp.zeros((d, 8), F32).at[:, 0].set(c_ctx).at[:, 1].set(c[0])
    mod = _ada(cvec_t, ada_w, ada_b)[:, :2].reshape(depth, 2, 6, d)

    h = jnp.concatenate([ctx[0], x[0]], axis=0)
    n_ctx = n_ctx_full
    cos_t, sin_t = _rope_tables(n_ctx_full, n_lat)

    a_next = None
    for i in range(depth):
        if i == depth - 1:
            h = h[n_ctx:]
            n_ctx = 0
        kind, j = i % N_MIXERS, i // N_MIXERS
        a = a_next if a_next is not None else _norm_mod(h, norm_mix[i], mod[i, :, 0:2], n_ctx)
        a_next = None
        gate1 = mod[i, :, 2]
        if kind == 0:
            b_gate, u = _proj_conv_in(a, conv_w_in, j)
            z = _conv_gate(b_gate, u, conv_w[j], n_ctx)
            h = _proj_resid(z, conv_w_out, j, h, gate1, n_ctx, "conv_out")
        elif kind == 1:
            qkv = _proj_plain(a, na_w_qkv, j, 0, 3 * d, "na_qkv")
            bias = _na_bias_tables(na_rpb[j], n_lat // GRID_W)
            o = _na_attention(qkv, bias, n_ctx)
            h = _proj_resid(o, na_w_out, j, h, gate1, n_ctx, "na_out")
        else:
            dq = GQA_HEADS * HEAD_DIM
            dkv = GQA_KV_HEADS * HEAD_DIM
            q = _proj_qk_rope(a, gqa_w_q, j, 0, dq, gqa_q_norm[j], cos_t, sin_t, HEAD_DIM ** -0.5 * LOG2_E, "gqa_q")
            k = _proj_qk_rope(a, gqa_w_kv, j, 0, dkv, gqa_k_norm[j], cos_t, sin_t, 1.0, "gqa_k")
            v = _proj_plain(a, gqa_w_kv, j, dkv, dkv, "gqa_v")
            o = _gqa_attention(q, k, v, n_ctx)
            h = _proj_resid(o, gqa_w_out, j, h, gate1, n_ctx, "gqa_out")
        gate2 = mod[i, :, 5]
        if i % 2 == 0:
            f = _norm_mod(h, norm_ffn[i], mod[i, :, 3:5], n_ctx)
            act = _proj_gu(f, ffn_w_gu, i // 2)
            h = _proj_resid(act, ffn_w_down, i // 2, h, gate2, n_ctx, "ffn_down")
        else:
            moe_args = (h, norm_ffn[i], mod[i, :, 3:5], gate2, moe_w_router, moe_w_gu, moe_w_down, i // 2, n_ctx)
            if i == depth - 1:
                return _moe(*moe_args, post="final", post_gain=norm_final)[None]
            if i + 1 < depth - 1:
                h, a_next = _moe(*moe_args, post="mod", post_gain=norm_mix[i + 1], post_ss=mod[i + 1, :, 0:2])
            else:
                h = _moe(*moe_args)
    return _final_norm(h, norm_final)[None]
```

```python
import functools

import numpy as np
import jax
import jax.numpy as jnp
from jax import lax
from jax.experimental import pallas as pl
from jax.experimental.pallas import tpu as pltpu

F32 = jnp.float32
BF16 = jnp.bfloat16
I32 = jnp.int32

EPS = 1e-6
GRID_W = 64
N_MIXERS = 3
NA_HEADS = 16
NA_KH = 8
NA_KW = 16
NA_QROWS = 4
NA_WROWS = NA_QROWS + NA_KH
GQA_HEADS = 16
GQA_KV_HEADS = 4
HEAD_DIM = 128
ROPE_THETA = 10000.0
N_EXPERTS = 8
TOP_K = 2
LANES = 128
ROW_BLOCK = 256
MASKED = -1e30
LOG2_E = float(np.log2(np.e))
VMEM_LIMIT = 60 * 1024 * 1024


def _cparams(*sem):
    return pltpu.CompilerParams(dimension_semantics=sem, vmem_limit_bytes=VMEM_LIMIT)


def _pick(n, cands):
    for c in cands:
        if n % c == 0:
            return c
    raise ValueError(f"no tile for {n} in {cands}")


def _dot(a, b):
    return jnp.dot(a, b, preferred_element_type=F32)


def _dot_nt(a, b):
    return lax.dot_general(a, b, (((1,), (1,)), ((), ())), preferred_element_type=F32)


def _silu(x):
    return x * (1.0 / (1.0 + jnp.exp(-x)))


def _rms(x, gain):
    return x * lax.rsqrt(jnp.mean(x * x, axis=-1, keepdims=True) + EPS) * gain


def _ada_kernel(ct_ref, w_ref, b_ref, o_ref, s_scr):
    @pl.when(jnp.logical_and(pl.program_id(0) == 0, pl.program_id(1) == 0))
    def _():
        st = _silu(ct_ref[...])
        for r in range(2):
            s_scr[r] = jnp.broadcast_to(st[:, r:r + 1], s_scr.shape[1:])

    tn = o_ref.shape[1]
    row = lax.broadcasted_iota(I32, (8, LANES), 0)
    for cb in range(tn // LANES):
        w = w_ref[:, cb * LANES:(cb + 1) * LANES]
        sums = [jnp.sum(w * s_scr[r], axis=0, keepdims=True) for r in range(2)]
        tile = jnp.where(row == 0, sums[0], jnp.where(row == 1, sums[1], 0.0))
        o_ref[:, cb * LANES:(cb + 1) * LANES] = tile + b_ref[:, cb * LANES:(cb + 1) * LANES]


def _ada(cvec_t, ada_w, ada_b):
    depth, d, n = ada_w.shape
    tn = _pick(n, (1024, 512, 256, 128))
    return pl.pallas_call(
        _ada_kernel,
        grid=(depth, n // tn),
        in_specs=[pl.BlockSpec((d, 8), lambda i, j: (0, 0)),
                  pl.BlockSpec((None, d, tn), lambda i, j: (i, 0, j)),
                  pl.BlockSpec((None, 1, tn), lambda i, j: (i, 0, j))],
        out_specs=pl.BlockSpec((None, 8, tn), lambda i, j: (i, 0, j)),
        out_shape=jax.ShapeDtypeStruct((depth, 8, n), F32),
        scratch_shapes=[pltpu.VMEM((2, d, LANES), F32)],
        compiler_params=_cparams("arbitrary", "arbitrary"),
        name="ada",
    )(cvec_t, ada_w, ada_b.reshape(depth, 1, n))


def _modulated(h_ref, g_ref, ss_ref, n_ctx):
    y = _rms(h_ref[...], g_ref[...])
    shift, scale = ss_ref[1, 0:1, :], ss_ref[1, 1:2, :]
    if n_ctx > 0:
        tm = h_ref.shape[0]
        is_ctx = pl.program_id(0) * tm + lax.broadcasted_iota(I32, (tm, 1), 0) < n_ctx
        shift = jnp.where(is_ctx, ss_ref[0, 0:1, :], shift)
        scale = jnp.where(is_ctx, ss_ref[0, 1:2, :], scale)
    return y * (1.0 + scale) + shift


def _norm_mod_kernel(h_ref, g_ref, ss_ref, a_ref, *, n_ctx):
    a_ref[...] = _modulated(h_ref, g_ref, ss_ref, n_ctx).astype(a_ref.dtype)


def _seg_map(n_ctx_blocks):
    return lambda i: (jnp.where(i < n_ctx_blocks, 0, 1), 0, 0)


def _norm_mod(h, gain, ss, n_ctx):
    m, d = h.shape
    tm = _row_tile(m)
    return pl.pallas_call(
        functools.partial(_norm_mod_kernel, n_ctx=n_ctx),
        grid=(m // tm,),
        in_specs=[pl.BlockSpec((tm, d), lambda i: (i, 0)),
                  pl.BlockSpec((1, d), lambda i: (0, 0)),
                  pl.BlockSpec((2, 2, d), lambda i: (0, 0, 0))],
        out_specs=pl.BlockSpec((tm, d), lambda i: (i, 0)),
        out_shape=jax.ShapeDtypeStruct((m, d), BF16),
        compiler_params=_cparams("arbitrary"),
        name="norm_mod",
    )(h, gain.reshape(1, d), ss)


def _router_kernel(h_ref, g_ref, ss_ref, wr_ref, ids_ref, wts_ref, *, n_ctx):
    f = _modulated(h_ref, g_ref, ss_ref, n_ctx)
    logits = jnp.dot(f, wr_ref[...], precision=lax.Precision.HIGHEST, preferred_element_type=F32)
    lane = lax.broadcasted_iota(I32, logits.shape, 1)
    neg = jnp.float32(-jnp.inf)
    l1 = jnp.where(lane < N_EXPERTS, logits, neg)
    m1 = jnp.max(l1, axis=-1, keepdims=True)
    i1 = jnp.min(jnp.where(l1 == m1, lane, LANES), axis=-1, keepdims=True)
    l2 = jnp.where(lane == i1, neg, l1)
    m2 = jnp.max(l2, axis=-1, keepdims=True)
    i2 = jnp.min(jnp.where(l2 == m2, lane, LANES), axis=-1, keepdims=True)
    e2 = jnp.exp(m2 - m1)
    w1 = 1.0 / (1.0 + e2)
    w2 = e2 / (1.0 + e2)
    ids_ref[...] = jnp.where(lane == 0, i1, jnp.where(lane == 1, i2, 0))
    wts_ref[...] = jnp.where(lane == 0, w1, jnp.where(lane == 1, w2, 0.0))


def _router(h, gain, ss, w_router, n_ctx):
    m, d = h.shape
    tm = _row_tile(m)
    wr = jnp.zeros((d, LANES), F32).at[:, :N_EXPERTS].set(w_router)
    return pl.pallas_call(
        functools.partial(_router_kernel, n_ctx=n_ctx),
        grid=(m // tm,),
        in_specs=[pl.BlockSpec((tm, d), lambda i: (i, 0)),
                  pl.BlockSpec((1, d), lambda i: (0, 0)),
                  pl.BlockSpec((2, 2, d), lambda i: (0, 0, 0)),
                  pl.BlockSpec((d, LANES), lambda i: (0, 0))],
        out_specs=[pl.BlockSpec((tm, LANES), lambda i: (i, 0)),
                   pl.BlockSpec((tm, LANES), lambda i: (i, 0))],
        out_shape=[jax.ShapeDtypeStruct((m, LANES), I32),
                   jax.ShapeDtypeStruct((m, LANES), F32)],
        compiler_params=_cparams("arbitrary"),
        name="router",
    )(h, gain.reshape(1, d), ss, wr)


def _final_norm_kernel(h_ref, g_ref, o_ref):
    o_ref[...] = _rms(h_ref[...], g_ref[...])


def _final_norm(h, gain):
    m, d = h.shape
    tm = ROW_BLOCK
    return pl.pallas_call(
        _final_norm_kernel,
        grid=(m // tm,),
        in_specs=[pl.BlockSpec((tm, d), lambda i: (i, 0)),
                  pl.BlockSpec((1, d), lambda i: (0, 0))],
        out_specs=pl.BlockSpec((tm, d), lambda i: (i, 0)),
        out_shape=jax.ShapeDtypeStruct((m, d), F32),
        compiler_params=_cparams("arbitrary"),
        name="final_norm",
    )(h, gain.reshape(1, d))


def _mm(x, w, *, sel, col_offs, n_tiles, tn, tm, epi, extra=(), extra_specs=(), out_shape, out_specs, name):
    m_rows, k = x.shape
    n_w = len(col_offs)
    n_ex = len(extra)
    n_out = len(out_shape)

    def kern(*refs):
        x_ref = refs[0]
        w_refs = refs[1:1 + n_w]
        ex = refs[1 + n_w:1 + n_w + n_ex]
        outs = refs[1 + n_w + n_ex:1 + n_w + n_ex + n_out]
        wb = refs[-1]
        m = pl.program_id(1)

        @pl.when(m == 0)
        def _():
            for j in range(n_w):
                wb[j] = w_refs[j][...].astype(BF16)

        xv = x_ref[...]
        accs = [_dot(xv, wb[j]) for j in range(n_w)]
        for o, r in zip(outs, epi(accs, m, *ex)):
            if isinstance(r, list):
                c0 = 0
                for piece in r:
                    o[:, c0:c0 + piece.shape[1]] = piece.astype(o.dtype)
                    c0 += piece.shape[1]
            else:
                o[...] = r.astype(o.dtype)

    def w_map(off):
        return lambda n, m: (sel, 0, n + off)

    in_specs = ([pl.BlockSpec((tm, k), lambda n, m: (m, 0))]
                + [pl.BlockSpec((None, k, tn), w_map(off)) for off in col_offs]
                + list(extra_specs))
    return pl.pallas_call(kern, grid=(n_tiles, m_rows // tm), in_specs=in_specs, out_specs=out_specs,
                          out_shape=out_shape, scratch_shapes=[pltpu.VMEM((n_w, k, tn), BF16)],
                          compiler_params=_cparams("arbitrary", "arbitrary"), name=name)(x, *([w] * n_w), *extra)


X_AHEAD = 3


def _mm_grouped(x, w, starts, counts, slab0, *, col_offs, n_tiles, tn, epi, out_dtype, n_out_cols, name):
    n_rows, k = x.shape
    n_w = len(col_offs)
    ts = MOE_TILE

    def kern(st_ref, ct_ref, s0_ref, x_hbm, *rest):
        w_refs = rest[:n_w]
        o_hbm = rest[n_w]
        xbuf, obuf, wb, pend, sin, sout = rest[n_w + 1:]
        n, e = pl.program_id(0), pl.program_id(1)
        base, cnt = st_ref[e], ct_ref[e]

        def x_copy_from(first_row, r, slot):
            row0 = pl.multiple_of(first_row + r * ts, ts)
            return pltpu.make_async_copy(x_hbm.at[pl.ds(row0, ts), :], xbuf.at[slot], sin.at[slot])

        def x_copy(r, slot):
            return x_copy_from(base, r, slot)

        def x_start_head(first_row, n_tiles_here, cond):
            for r0 in range(X_AHEAD):
                @pl.when(jnp.logical_and(cond, r0 < n_tiles_here))
                def _():
                    x_copy_from(first_row, r0, r0).start()

        def o_copy(r, slot):
            row0 = pl.multiple_of(base + r * ts, ts)
            col0 = pl.multiple_of(n * tn, tn)
            return pltpu.make_async_copy(obuf.at[slot], o_hbm.at[pl.ds(row0, ts), pl.ds(col0, tn)], sout.at[slot])

        def o_drain(slot):
            @pl.when(pend[slot] == 1)
            def _():
                o_copy(0, slot).wait()
                pend[slot] = 0

        def o_emit(r, slot, tile):
            o_drain(slot)
            obuf[slot] = tile
            o_copy(r, slot).start()
            pend[slot] = 1

        first_step = jnp.logical_and(n == 0, e == 0)
        last_step = jnp.logical_and(n == pl.num_programs(0) - 1, e == pl.num_programs(1) - 1)

        @pl.when(first_step)
        def _():
            pend[0] = 0
            pend[1] = 0

        x_start_head(base, cnt, first_step)

        for j in range(n_w):
            wb[j] = w_refs[j][...].astype(BF16)

        def body(r, carry):
            slot = r % (X_AHEAD + 1)

            @pl.when(r + X_AHEAD < cnt)
            def _():
                x_copy(r + X_AHEAD, (r + X_AHEAD) % (X_AHEAD + 1)).start()

            x_copy(r, slot).wait()
            xv = xbuf[slot]
            o_emit(r, r % 2, epi([_dot(xv, wb[j]) for j in range(n_w)]).astype(out_dtype))
            return carry

        lax.fori_loop(0, cnt, body, 0)

        e_next = jnp.where(e == pl.num_programs(1) - 1, 0, e + 1)
        x_start_head(st_ref[e_next], ct_ref[e_next], jnp.logical_not(last_step))

        @pl.when(e == pl.num_programs(1) - 1)
        def _():
            def zero_tile(r, carry):
                o_emit(r, r % 2, jnp.zeros((ts, tn), out_dtype))
                return carry

            lax.fori_loop(cnt, (n_rows - base) // ts, zero_tile, 0)

        @pl.when(last_step)
        def _():
            o_drain(0)
            o_drain(1)

    def w_map(off):
        return lambda n, e, st, ct, s0: (s0[0] + e, 0, n + off)

    gs = pltpu.PrefetchScalarGridSpec(
        num_scalar_prefetch=3,
        grid=(n_tiles, N_EXPERTS),
        in_specs=[pl.BlockSpec(memory_space=pl.ANY)] + [pl.BlockSpec((None, k, tn), w_map(off)) for off in col_offs],
        out_specs=pl.BlockSpec(memory_space=pl.ANY),
        scratch_shapes=[pltpu.VMEM((X_AHEAD + 1, ts, k), BF16), pltpu.VMEM((2, ts, tn), out_dtype),
                        pltpu.VMEM((n_w, k, tn), BF16), pltpu.SMEM((2,), I32),
                        pltpu.SemaphoreType.DMA((X_AHEAD + 1,)), pltpu.SemaphoreType.DMA((2,))])
    return pl.pallas_call(kern, grid_spec=gs, out_shape=jax.ShapeDtypeStruct((n_rows, n_out_cols), out_dtype),
                          compiler_params=_cparams("arbitrary", "arbitrary"),
                          name=name)(starts, counts, slab0, x, *([w] * n_w))


def _row_tile(m_rows):
    return _pick(m_rows, (768, 512, 256))


def _proj_plain(x, w, sel, col0, n_cols, name, lead_cols=0, lead_scale=1.0):
    m_rows = x.shape[0]
    tn = _pick(n_cols, (1024, 512, 256))
    assert col0 % tn == 0 and lead_cols % tn == 0
    tm = _row_tile(m_rows)

    def epi(accs, m):
        if lead_cols == 0:
            return (accs[0],)
        return (accs[0] * jnp.where(pl.program_id(0) < lead_cols // tn, lead_scale, 1.0),)

    return _mm(x, w, sel=sel, col_offs=(col0 // tn,), n_tiles=n_cols // tn, tn=tn, tm=tm,
               epi=epi,
               out_shape=[jax.ShapeDtypeStruct((m_rows, n_cols), BF16)],
               out_specs=[pl.BlockSpec((tm, tn), lambda n, m: (m, n))], name=name)[0]


def _proj_conv_in(x, w, sel):
    m_rows = x.shape[0]
    d3 = w.shape[2] // 3
    tn = _pick(d3, (512, 256))
    tm = _row_tile(m_rows)
    nt = d3 // tn
    o = jax.ShapeDtypeStruct((m_rows, d3), BF16)
    spec = pl.BlockSpec((tm, tn), lambda n, m: (m, n))
    return _mm(x, w, sel=sel, col_offs=(0, nt, 2 * nt), n_tiles=nt, tn=tn, tm=tm,
               epi=lambda accs, m: (accs[0], accs[1] * accs[2]),
               out_shape=[o, o], out_specs=[spec, spec], name="conv_in")


def _proj_gu(x, w, sel):
    m_rows = x.shape[0]
    f = w.shape[2] // 2
    tn = _pick(f, (512, 256))
    tm = _row_tile(m_rows)
    nt = f // tn
    return _mm(x, w, sel=sel, col_offs=(0, nt), n_tiles=nt, tn=tn, tm=tm,
               epi=lambda accs, m: (_silu(accs[0]) * accs[1],),
               out_shape=[jax.ShapeDtypeStruct((m_rows, f), BF16)],
               out_specs=[pl.BlockSpec((tm, tn), lambda n, m: (m, n))], name="ffn_gu")[0]


def _moe_gu(x, w, starts, counts, slab0):
    f = w.shape[2] // 2
    tn = _pick(f, (512, 256))
    nt = f // tn
    return _mm_grouped(x, w, starts, counts, slab0, col_offs=(0, nt), n_tiles=nt, tn=tn,
                       epi=lambda accs: _silu(accs[0]) * accs[1], out_dtype=BF16, n_out_cols=f, name="moe_gu")


def _moe_down(x, w, starts, counts, slab0):
    n_cols = w.shape[2]
    tn = 512
    return _mm_grouped(x, w, starts, counts, slab0, col_offs=(0,), n_tiles=n_cols // tn, tn=tn,
                       epi=lambda accs: accs[0], out_dtype=F32, n_out_cols=n_cols, name="moe_down")


def _proj_qk_rope(x, w, sel, col0, n_cols, gain, cos_t, sin_t, scale, name):
    m_rows = x.shape[0]
    tn = _pick(n_cols, (512, 256, 128))
    assert col0 % tn == 0
    tm = _row_tile(m_rows)

    def epi(accs, m, gain_ref, cos_ref, sin_ref):
        src = lax.broadcasted_iota(I32, (HEAD_DIM, HEAD_DIM), 0)
        dst = lax.broadcasted_iota(I32, (HEAD_DIM, HEAD_DIM), 1)
        perm = ((src ^ (HEAD_DIM // 4)) == dst).astype(BF16)
        cos_v, sin_v, g = cos_ref[...], sin_ref[...], gain_ref[...]
        heads = []
        for hh in range(tn // HEAD_DIM):
            y = _rms(accs[0][:, hh * HEAD_DIM:(hh + 1) * HEAD_DIM], g)
            y_hi = y.astype(BF16)
            y_lo = (y - y_hi.astype(F32)).astype(BF16)
            rot = _dot(y_hi, perm) + _dot(y_lo, perm)
            heads.append((y * cos_v + rot * sin_v) * scale)
        return (heads,)

    return _mm(x, w, sel=sel, col_offs=(col0 // tn,), n_tiles=n_cols // tn, tn=tn, tm=tm, epi=epi,
               extra=(gain.reshape(1, HEAD_DIM), cos_t, sin_t),
               extra_specs=(pl.BlockSpec((1, HEAD_DIM), lambda n, m: (0, 0)),
                            pl.BlockSpec((tm, HEAD_DIM), lambda n, m: (m, 0)),
                            pl.BlockSpec((tm, HEAD_DIM), lambda n, m: (m, 0))),
               out_shape=[jax.ShapeDtypeStruct((m_rows, n_cols), BF16)],
               out_specs=[pl.BlockSpec((tm, tn), lambda n, m: (m, n))], name=name)[0]


def _proj_resid(x, w, sel, h, gate, n_ctx, name):
    m_rows, k = x.shape
    n_cols = w.shape[2]
    big_k = k > 4096
    tn = 512 if big_k else _pick(n_cols, (1024, 512, 256))
    tm = _pick(m_rows, (512, 384, 256)) if big_k else _row_tile(m_rows)

    def epi(accs, m, h_ref, gate_ref):
        row = m * tm + lax.broadcasted_iota(I32, (tm, 1), 0)
        g = jnp.where(row < n_ctx, gate_ref[0:1, :], gate_ref[1:2, :])
        return (h_ref[...] + g * accs[0],)

    return _mm(x, w, sel=sel, col_offs=(0,), n_tiles=n_cols // tn, tn=tn, tm=tm, epi=epi,
               extra=(h, gate),
               extra_specs=(pl.BlockSpec((tm, tn), lambda n, m: (m, n)),
                            pl.BlockSpec((2, tn), lambda n, m: (0, n))),
               out_shape=[jax.ShapeDtypeStruct((m_rows, n_cols), F32)],
               out_specs=[pl.BlockSpec((tm, tn), lambda n, m: (m, n))], name=name)[0]


def _conv_gate_kernel(b_ref, u_ref, up_ref, un_ref, cw_ref, z_ref, *, tm, n_ctx_blocks, n_blocks):
    i = pl.program_id(0)
    u = u_ref[...].astype(F32)
    prev_ok = jnp.logical_and(i != 0, i != n_ctx_blocks).astype(F32)
    next_ok = jnp.logical_and(i != n_ctx_blocks - 1, i != n_blocks - 1).astype(F32)
    prow = up_ref[15:16, :].astype(F32) * prev_ok
    nrow = un_ref[0:1, :].astype(F32) * next_ok
    row = lax.broadcasted_iota(I32, (tm, 1), 0)
    um1 = jnp.where(row == 0, prow, pltpu.roll(u, 1, 0))
    up1 = jnp.where(row == tm - 1, nrow, pltpu.roll(u, tm - 1, 0))
    conv = cw_ref[0:1, :] * um1 + cw_ref[1:2, :] * u + cw_ref[2:3, :] * up1
    z_ref[...] = (b_ref[...].astype(F32) * conv).astype(z_ref.dtype)


def _conv_gate(b, u, conv_w, n_ctx):
    m, d = u.shape
    tm = ROW_BLOCK
    tc = _pick(d, (1024, 512, 256, 128))
    hb = 16
    nb = m // tm
    last_hb = m // hb - 1
    kern = functools.partial(_conv_gate_kernel, tm=tm, n_ctx_blocks=n_ctx // tm, n_blocks=nb)
    return pl.pallas_call(
        kern,
        grid=(nb, d // tc),
        in_specs=[pl.BlockSpec((tm, tc), lambda i, j: (i, j)),
                  pl.BlockSpec((tm, tc), lambda i, j: (i, j)),
                  pl.BlockSpec((hb, tc), lambda i, j: (jnp.maximum(i * (tm // hb) - 1, 0), j)),
                  pl.BlockSpec((hb, tc), lambda i, j: (jnp.minimum((i + 1) * (tm // hb), last_hb), j)),
                  pl.BlockSpec((3, tc), lambda i, j: (0, j))],
        out_specs=pl.BlockSpec((tm, tc), lambda i, j: (i, j)),
        out_shape=jax.ShapeDtypeStruct((m, d), BF16),
        compiler_params=_cparams("arbitrary", "arbitrary"),
        name="conv_gate",
    )(b, u, u, u, conv_w)


def _na_bias_tables(rpb, rows):
    nb = rows // NA_QROWS
    n_h = rpb.shape[0]
    c = np.arange(GRID_W)[:, None]
    kc = np.arange(GRID_W)[None, :]
    c0 = np.clip(c - NA_KW // 2, 0, GRID_W - NA_KW)
    c_valid = (kc >= c0) & (kc < c0 + NA_KW)
    c_sel = (kc - c + NA_KW - 1)[:, :, None] == np.arange(2 * NA_KW - 1)[None, None, :]
    c_sel = (c_sel & c_valid[:, :, None]).astype(np.float32)
    r_sel, r_valid = [], []
    for b in (0, 1, nb - 1):
        w0 = NA_QROWS * int(np.clip(b - 1, 0, nb - 3))
        ar = NA_QROWS * b + np.arange(NA_QROWS)[:, None]
        kr = w0 + np.arange(NA_WROWS)[None, :]
        r0 = np.clip(ar - NA_KH // 2, 0, rows - NA_KH)
        ok = (kr >= r0) & (kr < r0 + NA_KH)
        sel = (kr - ar + NA_KH - 1)[:, :, None] == np.arange(2 * NA_KH - 1)[None, None, :]
        r_sel.append((sel & ok[:, :, None]).astype(np.float32))
        r_valid.append(ok)
    r_sel, r_valid = np.stack(r_sel), np.stack(r_valid)
    t = jnp.einsum("pijr,hrd,ckd->phicjk", jnp.asarray(r_sel), rpb * LOG2_E, jnp.asarray(c_sel),
                   precision=lax.Precision.HIGHEST)
    valid = r_valid[:, None, :, None, :, None] & c_valid[None, None, None, :, None, :]
    t = jnp.where(valid, t, MASKED).reshape(3, n_h, NA_QROWS * GRID_W, NA_WROWS * GRID_W)
    return jnp.concatenate([jnp.full((1,) + t.shape[1:], MASKED, F32), t], axis=0)


def _na_kernel(q_ref, k0_ref, k1_ref, k2_ref, kc_ref, v0_ref, v1_ref, v2_ref, vc_ref, bias_ref, o_ref):
    blk = ROW_BLOCK
    ones = jnp.ones((blk, HEAD_DIM), BF16)
    for h in range(NA_HEADS):
        hs = slice(h * HEAD_DIM, (h + 1) * HEAD_DIM)
        q = q_ref[:, hs]
        ss = [_dot_nt(q, kr[:, hs]) + bias_ref[h, :, j * blk:(j + 1) * blk]
              for j, kr in enumerate((k0_ref, k1_ref, k2_ref))]
        ss.append(_dot_nt(q, kc_ref[:, hs]))
        mx = jnp.max(functools.reduce(jnp.maximum, ss), axis=-1, keepdims=True)
        acc = functools.reduce(jnp.add, [
            _dot(jnp.exp2(s - mx).astype(BF16), jnp.concatenate([vr[:, hs], ones], axis=1))
            for s, vr in zip(ss, (v0_ref, v1_ref, v2_ref, vc_ref))])
        o_ref[:, hs] = (acc[:, :HEAD_DIM] / acc[:, HEAD_DIM:]).astype(o_ref.dtype)


def _na_attention(qkv, bias, n_ctx):
    m, d3 = qkv.shape
    d = d3 // 3
    blk = ROW_BLOCK
    assert n_ctx == blk and NA_QROWS * GRID_W == blk
    nq = m // blk
    nb = nq - 1

    def kv_map(j, col):
        return lambda g: (1 + jnp.clip(g - 2, 0, nb - 3) + j, col)

    def bias_map(g):
        return (jnp.where(g == 0, 0, jnp.where(g == 1, 1, jnp.where(g == nq - 1, 3, 2))), 0, 0, 0)

    blkspec = lambda imap: pl.BlockSpec((blk, d), imap)
    return pl.pallas_call(
        _na_kernel,
        grid=(nq,),
        in_specs=[blkspec(lambda g: (g, 0)),
                  blkspec(kv_map(0, 1)), blkspec(kv_map(1, 1)), blkspec(kv_map(2, 1)), blkspec(lambda g: (0, 1)),
                  blkspec(kv_map(0, 2)), blkspec(kv_map(1, 2)), blkspec(kv_map(2, 2)), blkspec(lambda g: (0, 2)),
                  pl.BlockSpec((None, NA_HEADS, blk, 3 * blk), bias_map, pipeline_mode=pl.Buffered(1))],
        out_specs=blkspec(lambda g: (g, 0)),
        out_shape=jax.ShapeDtypeStruct((m, d), BF16),
        compiler_params=_cparams("arbitrary"),
        name="na_attention",
    )(qkv, qkv, qkv, qkv, qkv, qkv, qkv, qkv, qkv, bias)


def _rope_tables(n_ctx, n_lat):
    t = jnp.arange(n_lat)
    row = (t // GRID_W).astype(F32)
    col = (t % GRID_W).astype(F32)
    quarter = HEAD_DIM // 4
    inv_freq = ROPE_THETA ** (-jnp.arange(quarter, dtype=F32) / quarter)
    ar, ac = row[:, None] * inv_freq, col[:, None] * inv_freq
    cos_t = jnp.concatenate([jnp.cos(ar), jnp.cos(ar), jnp.cos(ac), jnp.cos(ac)], axis=1)
    sin_t = jnp.concatenate([-jnp.sin(ar), jnp.sin(ar), -jnp.sin(ac), jnp.sin(ac)], axis=1)
    cos_t = jnp.concatenate([jnp.ones((n_ctx, HEAD_DIM), F32), cos_t], axis=0)
    sin_t = jnp.concatenate([jnp.zeros((n_ctx, HEAD_DIM), F32), sin_t], axis=0)
    return cos_t, sin_t


def _gqa_kernel(q_ref, k_ref, v_ref, o_ref, *, n_ctx, group):
    def attend(n_keys):
        qs = [q_ref[:, g * HEAD_DIM:(g + 1) * HEAD_DIM] for g in range(group)]
        mx, acc = [None] * group, [None] * group
        for c in range(n_keys // ROW_BLOCK):
            kk = k_ref[c * ROW_BLOCK:(c + 1) * ROW_BLOCK, :]
            vv = v_ref[c * ROW_BLOCK:(c + 1) * ROW_BLOCK, :]
            for g in range(group):
                s = _dot_nt(qs[g], kk)
                row_max = jnp.max(s, axis=-1, keepdims=True)
                if c == 0:
                    mx[g] = row_max
                    acc[g] = _dot(jnp.exp2(s - row_max).astype(BF16), vv)
                else:
                    mx_new = jnp.maximum(mx[g], row_max)
                    acc[g] = jnp.exp2(mx[g] - mx_new) * acc[g] + _dot(jnp.exp2(s - mx_new).astype(BF16), vv)
                    mx[g] = mx_new
        for g in range(group):
            o_ref[:, g * HEAD_DIM:(g + 1) * HEAD_DIM] = (
                acc[g][:, :HEAD_DIM] / acc[g][:, HEAD_DIM:]).astype(o_ref.dtype)

    is_ctx = pl.program_id(1) * ROW_BLOCK < n_ctx

    @pl.when(is_ctx)
    def _():
        attend(n_ctx)

    @pl.when(jnp.logical_not(is_ctx))
    def _():
        attend(k_ref.shape[0])


def _gqa_attention(q, k, v, n_ctx):
    m, dq = q.shape
    group = GQA_HEADS // GQA_KV_HEADS
    gw = group * HEAD_DIM
    tq = ROW_BLOCK
    kern = functools.partial(_gqa_kernel, n_ctx=n_ctx, group=group)
    v = jnp.concatenate([v.reshape(m, GQA_KV_HEADS, HEAD_DIM), jnp.ones((m, GQA_KV_HEADS, HEAD_DIM), v.dtype)],
                        axis=2).reshape(m, 2 * GQA_KV_HEADS * HEAD_DIM)
    return pl.pallas_call(
        kern,
        grid=(GQA_KV_HEADS, m // tq),
        in_specs=[pl.BlockSpec((tq, gw), lambda kh, i: (i, kh)),
                  pl.BlockSpec((m, HEAD_DIM), lambda kh, i: (0, kh)),
                  pl.BlockSpec((m, 2 * HEAD_DIM), lambda kh, i: (0, kh))],
        out_specs=pl.BlockSpec((tq, gw), lambda kh, i: (i, kh)),
        out_shape=jax.ShapeDtypeStruct((m, dq), BF16),
        compiler_params=_cparams("arbitrary", "arbitrary"),
        name="gqa_attention",
    )(q, k, v)


MOE_TILE = 256


def _moe_plan(ids):
    m = ids.shape[0]
    tm = MOE_TILE
    n_rows = -(-(TOP_K * m + N_EXPERTS * (tm - 1)) // tm) * tm
    flat_e = ids.reshape(-1)
    onehot = (flat_e[:, None] == jnp.arange(N_EXPERTS, dtype=I32)[None, :]).astype(I32)
    csum = jnp.cumsum(onehot, axis=0)
    rank = jnp.sum(csum * onehot, axis=1) - 1
    tiles = (csum[-1] + tm - 1) // tm
    starts = (jnp.cumsum(tiles) - tiles) * tm
    pos = jnp.sum(starts[None, :] * onehot, axis=1) + rank
    tok = jnp.arange(TOP_K * m, dtype=I32) // TOP_K
    src = jnp.zeros((n_rows,), I32).at[pos].set(tok)
    return src, pos.astype(I32), starts.astype(I32), tiles.astype(I32)


def _prefetch_rows(idx_ref, src_hbm, buf, sem, i, n_steps, n_rows):
    def row_copy(slot, r, row):
        return pltpu.make_async_copy(src_hbm.at[pl.ds(row, 1), :], buf.at[slot, pl.ds(r, 1), :], sem.at[slot])

    def start_all(step):
        slot = step % 2
        for r in range(n_rows):
            row_copy(slot, r, idx_ref[step * n_rows + r]).start(priority=r % 2)

    @pl.when(i == 0)
    def _():
        start_all(i)

    @pl.when(i + 1 < n_steps)
    def _():
        start_all(i + 1)

    for r in range(n_rows):
        row_copy(i % 2, r, 0).wait()


def _gather_norm_kernel(src_ref, tok_ref, h_hbm, g_ref, ssc_ref, ssl_ref, o_ref, buf, sem, *, tg, n_ctx):
    i = pl.program_id(0)
    _prefetch_rows(src_ref, h_hbm, buf, sem, i, pl.num_programs(0), tg)
    y = _rms(buf[i % 2], g_ref[...])
    scale, shift = ssl_ref[1:2, :], ssl_ref[0:1, :]
    if n_ctx > 0:
        is_ctx = tok_ref[...] < n_ctx
        scale = jnp.where(is_ctx, ssc_ref[1:2, :], scale)
        shift = jnp.where(is_ctx, ssc_ref[0:1, :], shift)
    o_ref[...] = (y * (1.0 + scale) + shift).astype(o_ref.dtype)


def _gather_norm(h, src, gain, ss, n_ctx):
    m, d = h.shape
    n_rows = src.shape[0]
    tg = ROW_BLOCK
    kern = functools.partial(_gather_norm_kernel, tg=tg, n_ctx=n_ctx)
    gs = pltpu.PrefetchScalarGridSpec(
        num_scalar_prefetch=1,
        grid=(n_rows // tg,),
        in_specs=[pl.BlockSpec((tg, 1), lambda i, *pf: (i, 0)),
                  pl.BlockSpec(memory_space=pl.ANY),
                  pl.BlockSpec((1, d), lambda i, *pf: (0, 0)),
                  pl.BlockSpec((None, 2, d), lambda i, *pf: (0, 0, 0)),
                  pl.BlockSpec((None, 2, d), lambda i, *pf: (1, 0, 0))],
        out_specs=pl.BlockSpec((tg, d), lambda i, *pf: (i, 0)),
        scratch_shapes=[pltpu.VMEM((2, tg, d), F32), pltpu.SemaphoreType.DMA((2,))])
    return pl.pallas_call(kern, grid_spec=gs, out_shape=jax.ShapeDtypeStruct((n_rows, d), BF16),
                          compiler_params=_cparams("arbitrary"),
                          name="moe_gather")(src, src.reshape(n_rows, 1), h, gain.reshape(1, d), ss, ss)


def _combine_kernel(pos_ref, h_ref, gate_ref, wts_ref, y_hbm, *rest, tg, n_ctx, post):
    n_out = _N_COMBINE_OUTS[post]
    buf, sem = rest[-2:]
    outs = rest[-2 - n_out:-2]
    post_refs = rest[:-2 - n_out]
    i = pl.program_id(0)
    _prefetch_rows(pos_ref, y_hbm, buf, sem, i, pl.num_programs(0), tg * TOP_K)
    row = i * tg + lax.broadcasted_iota(I32, (tg, 1), 0)
    g = jnp.where(row < n_ctx, gate_ref[0:1, :], gate_ref[1:2, :])
    slot = i % 2
    mix = wts_ref[:, 0:1] * buf[slot, 0:tg, :] + wts_ref[:, 1:2] * buf[slot, tg:TOP_K * tg, :]
    h_new = h_ref[...] + g * mix
    if post == "final":
        outs[0][...] = _rms(h_new, post_refs[0][...])
        return
    outs[0][...] = h_new
    if post == "mod":
        ss_ref = post_refs[1]
        outs[1][...] = (_rms(h_new, post_refs[0][...]) * (1.0 + ss_ref[1:2, :]) + ss_ref[0:1, :]).astype(BF16)


_N_COMBINE_OUTS = {None: 1, "final": 1, "mod": 2}


def _combine(h, y, pos, wts, gate, n_ctx, post=None, post_gain=None, post_ss=None):
    m, d = h.shape
    tg = ROW_BLOCK
    pos_tiles = pos.reshape(m // tg, tg, TOP_K).transpose(0, 2, 1).reshape(-1)
    kern = functools.partial(_combine_kernel, tg=tg, n_ctx=n_ctx, post=post)
    row_spec = pl.BlockSpec((tg, d), lambda i, *pf: (i, 0))
    in_specs = [row_spec,
                pl.BlockSpec((2, d), lambda i, *pf: (0, 0)),
                pl.BlockSpec((tg, LANES), lambda i, *pf: (i, 0)),
                pl.BlockSpec(memory_space=pl.ANY)]
    args = [pos_tiles, h, gate, wts, y]
    out_shape = [jax.ShapeDtypeStruct((m, d), F32)]
    if post is not None:
        in_specs.append(pl.BlockSpec((1, d), lambda i, *pf: (0, 0)))
        args.append(post_gain.reshape(1, d))
    if post == "mod":
        seg = _seg_map(n_ctx // tg)
        in_specs.append(pl.BlockSpec((None, 2, d), lambda i, *pf: seg(i)))
        args.append(post_ss)
        out_shape.append(jax.ShapeDtypeStruct((m, d), BF16))
    gs = pltpu.PrefetchScalarGridSpec(
        num_scalar_prefetch=1,
        grid=(m // tg,),
        in_specs=in_specs,
        out_specs=[row_spec] * len(out_shape),
        scratch_shapes=[pltpu.VMEM((2, TOP_K * tg, d), F32), pltpu.SemaphoreType.DMA((2,))])
    res = pl.pallas_call(kern, grid_spec=gs, out_shape=out_shape,
                         compiler_params=_cparams("arbitrary"), name="moe_combine")(*args)
    return res if post == "mod" else res[0]


def _moe(h, gain, ss, gate, w_router, w_gu, w_down, j, n_ctx, **post):
    ids, wts = _router(h, gain, ss, w_router[j], n_ctx)
    src, pos, starts, tiles = _moe_plan(ids[:, :TOP_K])
    xs = _gather_norm(h, src, gain, ss, n_ctx)
    slab0 = jnp.full((1,), j * N_EXPERTS, I32)
    act = _moe_gu(xs, w_gu.reshape((-1,) + w_gu.shape[2:]), starts, tiles, slab0)
    y = _moe_down(act, w_down.reshape((-1,) + w_down.shape[2:]), starts, tiles, slab0)
    return _combine(h, y, pos, wts, gate, n_ctx, **post)


def kernel(x, c, ctx, c_ctx, ada_w, ada_b, norm_mix, norm_ffn, norm_final, conv_w_in, conv_w, conv_w_out,
           na_w_qkv, na_rpb, na_w_out, gqa_w_q, gqa_w_kv, gqa_q_norm, gqa_k_norm, gqa_w_out, ffn_w_gu,
           ffn_w_down, moe_w_router, moe_w_gu, moe_w_down):
    bsz, n_lat, d = x.shape
    n_ctx_full = ctx.shape[1]
    depth = ada_w.shape[0]
    assert bsz == 1 and n_ctx_full == ROW_BLOCK and n_lat % (GRID_W * NA_QROWS) == 0
    assert n_lat // GRID_W >= NA_WROWS

    cvec_t = jnp.zeros((d, 8), F32).at[:, 0].set(c_ctx).at[:, 1].set(c[0])
    mod = _ada(cvec_t, ada_w, ada_b)[:, :2].reshape(depth, 2, 6, d)

    h = jnp.concatenate([ctx[0], x[0]], axis=0)
    n_ctx = n_ctx_full
    cos_t, sin_t = _rope_tables(n_ctx_full, n_lat)

    a_next = None
    for i in range(depth):
        if i == depth - 1:
            h = h[n_ctx:]
            n_ctx = 0
        kind, j = i % N_MIXERS, i // N_MIXERS
        a = a_next if a_next is not None else _norm_mod(h, norm_mix[i], mod[i, :, 0:2], n_ctx)
        a_next = None
        gate1 = mod[i, :, 2]
        if kind == 0:
            b_gate, u = _proj_conv_in(a, conv_w_in, j)
            z = _conv_gate(b_gate, u, conv_w[j], n_ctx)
            h = _proj_resid(z, conv_w_out, j, h, gate1, n_ctx, "conv_out")
        elif kind == 1:
            qkv = _proj_plain(a, na_w_qkv, j, 0, 3 * d, "na_qkv", lead_cols=d, lead_scale=HEAD_DIM ** -0.5 * LOG2_E)
            bias = _na_bias_tables(na_rpb[j], n_lat // GRID_W)
            o = _na_attention(qkv, bias, n_ctx)
            h = _proj_resid(o, na_w_out, j, h, gate1, n_ctx, "na_out")
        else:
            dq = GQA_HEADS * HEAD_DIM
            dkv = GQA_KV_HEADS * HEAD_DIM
            q = _proj_qk_rope(a, gqa_w_q, j, 0, dq, gqa_q_norm[j], cos_t, sin_t, HEAD_DIM ** -0.5 * LOG2_E, "gqa_q")
            k = _proj_qk_rope(a, gqa_w_kv, j, 0, dkv, gqa_k_norm[j], cos_t, sin_t, 1.0, "gqa_k")
            v = _proj_plain(a, gqa_w_kv, j, dkv, dkv, "gqa_v")
            o = _gqa_attention(q, k, v, n_ctx)
            h = _proj_resid(o, gqa_w_out, j, h, gate1, n_ctx, "gqa_out")
        gate2 = mod[i, :, 5]
        if i % 2 == 0:
            f = _norm_mod(h, norm_ffn[i], mod[i, :, 3:5], n_ctx)
            act = _proj_gu(f, ffn_w_gu, i // 2)
            h = _proj_resid(act, ffn_w_down, i // 2, h, gate2, n_ctx, "ffn_down")
        else:
            moe_args = (h, norm_ffn[i], mod[i, :, 3:5], gate2, moe_w_router, moe_w_gu, moe_w_down, i // 2, n_ctx)
            if i == depth - 1:
                return _moe(*moe_args, post="final", post_gain=norm_final)[None]
            if i + 1 < depth - 1:
                h, a_next = _moe(*moe_args, post="mod", post_gain=norm_mix[i + 1], post_ss=mod[i + 1, :, 0:2])
            else:
                h = _moe(*moe_args)
    return _final_norm(h, norm_final)[None]
```

```python
import functools

import numpy as np
import jax
import jax.numpy as jnp
from jax import lax
from jax.experimental import pallas as pl
from jax.experimental.pallas import tpu as pltpu

F32 = jnp.float32
BF16 = jnp.bfloat16
I32 = jnp.int32

EPS = 1e-6
GRID_W = 64
N_MIXERS = 3
NA_HEADS = 16
NA_KH = 8
NA_KW = 16
NA_QROWS = 4
NA_WROWS = NA_QROWS + NA_KH
GQA_HEADS = 16
GQA_KV_HEADS = 4
HEAD_DIM = 128
ROPE_THETA = 10000.0
N_EXPERTS = 8
TOP_K = 2
LANES = 128
ROW_BLOCK = 256
MASKED = -1e30
LOG2_E = float(np.log2(np.e))
VMEM_LIMIT = 60 * 1024 * 1024


def _cparams(*sem):
    return pltpu.CompilerParams(dimension_semantics=sem, vmem_limit_bytes=VMEM_LIMIT)


def _pick(n, cands):
    for c in cands:
        if n % c == 0:
            return c
    raise ValueError(f"no tile for {n} in {cands}")


def _dot(a, b):
    return jnp.dot(a, b, preferred_element_type=F32)


def _dot_nt(a, b):
    return lax.dot_general(a, b, (((1,), (1,)), ((), ())), preferred_element_type=F32)


def _silu(x):
    return x * (1.0 / (1.0 + jnp.exp(-x)))


def _rms(x, gain):
    return x * lax.rsqrt(jnp.mean(x * x, axis=-1, keepdims=True) + EPS) * gain


def _ada_kernel(ct_ref, w_ref, b_ref, o_ref, s_scr):
    @pl.when(jnp.logical_and(pl.program_id(0) == 0, pl.program_id(1) == 0))
    def _():
        st = _silu(ct_ref[...])
        for r in range(2):
            s_scr[r] = jnp.broadcast_to(st[:, r:r + 1], s_scr.shape[1:])

    tn = o_ref.shape[1]
    row = lax.broadcasted_iota(I32, (8, LANES), 0)
    for cb in range(tn // LANES):
        w = w_ref[:, cb * LANES:(cb + 1) * LANES]
        sums = [jnp.sum(w * s_scr[r], axis=0, keepdims=True) for r in range(2)]
        tile = jnp.where(row == 0, sums[0], jnp.where(row == 1, sums[1], 0.0))
        o_ref[:, cb * LANES:(cb + 1) * LANES] = tile + b_ref[:, cb * LANES:(cb + 1) * LANES]


def _ada(cvec_t, ada_w, ada_b):
    depth, d, n = ada_w.shape
    tn = _pick(n, (1024, 512, 256, 128))
    return pl.pallas_call(
        _ada_kernel,
        grid=(depth, n // tn),
        in_specs=[pl.BlockSpec((d, 8), lambda i, j: (0, 0)),
                  pl.BlockSpec((None, d, tn), lambda i, j: (i, 0, j)),
                  pl.BlockSpec((None, 1, tn), lambda i, j: (i, 0, j))],
        out_specs=pl.BlockSpec((None, 8, tn), lambda i, j: (i, 0, j)),
        out_shape=jax.ShapeDtypeStruct((depth, 8, n), F32),
        scratch_shapes=[pltpu.VMEM((2, d, LANES), F32)],
        compiler_params=_cparams("arbitrary", "arbitrary"),
        name="ada",
    )(cvec_t, ada_w, ada_b.reshape(depth, 1, n))


def _with_modulated(h_ref, g_ref, ss_ref, n_ctx, emit):
    x = h_ref[...]
    xr = x * lax.rsqrt(jnp.mean(x * x, axis=-1, keepdims=True) + EPS)
    g = g_ref[...]
    gs_lat, sh_lat = g * (1.0 + ss_ref[1, 1:2, :]), ss_ref[1, 0:1, :]
    if n_ctx == 0:
        emit(xr * gs_lat + sh_lat)
        return
    tm = h_ref.shape[0]
    row0 = pl.program_id(0) * tm

    @pl.when(row0 < n_ctx)
    def _():
        is_ctx = row0 + lax.broadcasted_iota(I32, (tm, 1), 0) < n_ctx
        gs = jnp.where(is_ctx, g * (1.0 + ss_ref[0, 1:2, :]), gs_lat)
        emit(xr * gs + jnp.where(is_ctx, ss_ref[0, 0:1, :], sh_lat))

    @pl.when(row0 >= n_ctx)
    def _():
        emit(xr * gs_lat + sh_lat)


def _norm_mod_kernel(h_ref, g_ref, ss_ref, a_ref, *, n_ctx):
    def emit(f):
        a_ref[...] = f.astype(a_ref.dtype)

    _with_modulated(h_ref, g_ref, ss_ref, n_ctx, emit)


def _seg_map(n_ctx_blocks):
    return lambda i: (jnp.where(i < n_ctx_blocks, 0, 1), 0, 0)


def _norm_mod(h, gain, ss, n_ctx):
    m, d = h.shape
    tm = _row_tile(m)
    return pl.pallas_call(
        functools.partial(_norm_mod_kernel, n_ctx=n_ctx),
        grid=(m // tm,),
        in_specs=[pl.BlockSpec((tm, d), lambda i: (i, 0)),
                  pl.BlockSpec((1, d), lambda i: (0, 0)),
                  pl.BlockSpec((2, 2, d), lambda i: (0, 0, 0))],
        out_specs=pl.BlockSpec((tm, d), lambda i: (i, 0)),
        out_shape=jax.ShapeDtypeStruct((m, d), BF16),
        compiler_params=_cparams("arbitrary"),
        name="norm_mod",
    )(h, gain.reshape(1, d), ss)


def _router_kernel(h_ref, g_ref, ss_ref, wr_ref, ids_ref, wts_ref, *, n_ctx):
    def emit(f):
        logits = jnp.dot(f, wr_ref[...], precision=lax.Precision.HIGHEST, preferred_element_type=F32)
        lane = lax.broadcasted_iota(I32, logits.shape, 1)
        neg = jnp.float32(-jnp.inf)
        l1 = jnp.where(lane < N_EXPERTS, logits, neg)
        m1 = jnp.max(l1, axis=-1, keepdims=True)
        i1 = jnp.min(jnp.where(l1 == m1, lane, LANES), axis=-1, keepdims=True)
        l2 = jnp.where(lane == i1, neg, l1)
        m2 = jnp.max(l2, axis=-1, keepdims=True)
        i2 = jnp.min(jnp.where(l2 == m2, lane, LANES), axis=-1, keepdims=True)
        e2 = jnp.exp(m2 - m1)
        w1 = 1.0 / (1.0 + e2)
        w2 = e2 / (1.0 + e2)
        ids_ref[...] = jnp.where(lane == 0, i1, jnp.where(lane == 1, i2, 0))
        wts_ref[...] = jnp.where(lane == 0, w1, jnp.where(lane == 1, w2, 0.0))

    _with_modulated(h_ref, g_ref, ss_ref, n_ctx, emit)


def _router(h, gain, ss, w_router, n_ctx):
    m, d = h.shape
    tm = _row_tile(m)
    wr = jnp.zeros((d, LANES), F32).at[:, :N_EXPERTS].set(w_router)
    return pl.pallas_call(
        functools.partial(_router_kernel, n_ctx=n_ctx),
        grid=(m // tm,),
        in_specs=[pl.BlockSpec((tm, d), lambda i: (i, 0)),
                  pl.BlockSpec((1, d), lambda i: (0, 0)),
                  pl.BlockSpec((2, 2, d), lambda i: (0, 0, 0)),
                  pl.BlockSpec((d, LANES), lambda i: (0, 0))],
        out_specs=[pl.BlockSpec((tm, LANES), lambda i: (i, 0)),
                   pl.BlockSpec((tm, LANES), lambda i: (i, 0))],
        out_shape=[jax.ShapeDtypeStruct((m, LANES), I32),
                   jax.ShapeDtypeStruct((m, LANES), F32)],
        compiler_params=_cparams("arbitrary"),
        name="router",
    )(h, gain.reshape(1, d), ss, wr)


def _final_norm_kernel(h_ref, g_ref, o_ref):
    o_ref[...] = _rms(h_ref[...], g_ref[...])


def _final_norm(h, gain):
    m, d = h.shape
    tm = ROW_BLOCK
    return pl.pallas_call(
        _final_norm_kernel,
        grid=(m // tm,),
        in_specs=[pl.BlockSpec((tm, d), lambda i: (i, 0)),
                  pl.BlockSpec((1, d), lambda i: (0, 0))],
        out_specs=pl.BlockSpec((tm, d), lambda i: (i, 0)),
        out_shape=jax.ShapeDtypeStruct((m, d), F32),
        compiler_params=_cparams("arbitrary"),
        name="final_norm",
    )(h, gain.reshape(1, d))


def _mm(x, w, *, sel, col_offs, n_tiles, tn, tm, epi, extra=(), extra_specs=(), out_shape, out_specs, name):
    m_rows, k = x.shape
    n_w = len(col_offs)
    n_ex = len(extra)
    n_out = len(out_shape)

    def kern(*refs):
        x_ref = refs[0]
        w_refs = refs[1:1 + n_w]
        ex = refs[1 + n_w:1 + n_w + n_ex]
        outs = refs[1 + n_w + n_ex:1 + n_w + n_ex + n_out]
        wb = refs[-1]
        m = pl.program_id(1)

        @pl.when(m == 0)
        def _():
            for j in range(n_w):
                wb[j] = w_refs[j][...].astype(BF16)

        xv = x_ref[...]
        accs = [_dot(xv, wb[j]) for j in range(n_w)]
        for o, r in zip(outs, epi(accs, m, *ex)):
            if isinstance(r, list):
                c0 = 0
                for piece in r:
                    o[:, c0:c0 + piece.shape[1]] = piece.astype(o.dtype)
                    c0 += piece.shape[1]
            else:
                o[...] = r.astype(o.dtype)

    def w_map(off):
        return lambda n, m: (sel, 0, n + off)

    in_specs = ([pl.BlockSpec((tm, k), lambda n, m: (m, 0))]
                + [pl.BlockSpec((None, k, tn), w_map(off)) for off in col_offs]
                + list(extra_specs))
    return pl.pallas_call(kern, grid=(n_tiles, m_rows // tm), in_specs=in_specs, out_specs=out_specs,
                          out_shape=out_shape, scratch_shapes=[pltpu.VMEM((n_w, k, tn), BF16)],
                          compiler_params=_cparams("arbitrary", "arbitrary"), name=name)(x, *([w] * n_w), *extra)


X_AHEAD = 3


def _mm_grouped(x, w, starts, counts, slab0, *, col_offs, n_tiles, tn, epi, out_dtype, n_out_cols, name):
    n_rows, k = x.shape
    n_w = len(col_offs)
    ts = MOE_TILE

    def kern(st_ref, ct_ref, s0_ref, x_hbm, *rest):
        w_refs = rest[:n_w]
        o_hbm = rest[n_w]
        xbuf, obuf, wb, pend, sin, sout = rest[n_w + 1:]
        n, e = pl.program_id(0), pl.program_id(1)
        base, cnt = st_ref[e], ct_ref[e]

        def x_copy_from(first_row, r, slot):
            row0 = pl.multiple_of(first_row + r * ts, ts)
            return pltpu.make_async_copy(x_hbm.at[pl.ds(row0, ts), :], xbuf.at[slot], sin.at[slot])

        def x_copy(r, slot):
            return x_copy_from(base, r, slot)

        def x_start_head(first_row, n_tiles_here, cond):
            for r0 in range(X_AHEAD):
                @pl.when(jnp.logical_and(cond, r0 < n_tiles_here))
                def _():
                    x_copy_from(first_row, r0, r0).start()

        def o_copy(r, slot):
            row0 = pl.multiple_of(base + r * ts, ts)
            col0 = pl.multiple_of(n * tn, tn)
            return pltpu.make_async_copy(obuf.at[slot], o_hbm.at[pl.ds(row0, ts), pl.ds(col0, tn)], sout.at[slot])

        def o_drain(slot):
            @pl.when(pend[slot] == 1)
            def _():
                o_copy(0, slot).wait()
                pend[slot] = 0

        def o_emit(r, slot, tile):
            o_drain(slot)
            obuf[slot] = tile
            o_copy(r, slot).start()
            pend[slot] = 1

        first_step = jnp.logical_and(n == 0, e == 0)
        last_step = jnp.logical_and(n == pl.num_programs(0) - 1, e == pl.num_programs(1) - 1)

        @pl.when(first_step)
        def _():
            pend[0] = 0
            pend[1] = 0

        x_start_head(base, cnt, first_step)

        for j in range(n_w):
            wb[j] = w_refs[j][...].astype(BF16)

        def body(r, carry):
            slot = r % (X_AHEAD + 1)

            @pl.when(r + X_AHEAD < cnt)
            def _():
                x_copy(r + X_AHEAD, (r + X_AHEAD) % (X_AHEAD + 1)).start()

            x_copy(r, slot).wait()
            xv = xbuf[slot]
            o_emit(r, r % 2, epi([_dot(xv, wb[j]) for j in range(n_w)]).astype(out_dtype))
            return carry

        lax.fori_loop(0, cnt, body, 0)

        e_next = jnp.where(e == pl.num_programs(1) - 1, 0, e + 1)
        x_start_head(st_ref[e_next], ct_ref[e_next], jnp.logical_not(last_step))

        @pl.when(e == pl.num_programs(1) - 1)
        def _():
            def zero_tile(r, carry):
                o_emit(r, r % 2, jnp.zeros((ts, tn), out_dtype))
                return carry

            lax.fori_loop(cnt, (n_rows - base) // ts, zero_tile, 0)

        @pl.when(last_step)
        def _():
            o_drain(0)
            o_drain(1)

    def w_map(off):
        return lambda n, e, st, ct, s0: (s0[0] + e, 0, n + off)

    gs = pltpu.PrefetchScalarGridSpec(
        num_scalar_prefetch=3,
        grid=(n_tiles, N_EXPERTS),
        in_specs=[pl.BlockSpec(memory_space=pl.ANY)] + [pl.BlockSpec((None, k, tn), w_map(off)) for off in col_offs],
        out_specs=pl.BlockSpec(memory_space=pl.ANY),
        scratch_shapes=[pltpu.VMEM((X_AHEAD + 1, ts, k), BF16), pltpu.VMEM((2, ts, tn), out_dtype),
                        pltpu.VMEM((n_w, k, tn), BF16), pltpu.SMEM((2,), I32),
                        pltpu.SemaphoreType.DMA((X_AHEAD + 1,)), pltpu.SemaphoreType.DMA((2,))])
    return pl.pallas_call(kern, grid_spec=gs, out_shape=jax.ShapeDtypeStruct((n_rows, n_out_cols), out_dtype),
                          compiler_params=_cparams("arbitrary", "arbitrary"),
                          name=name)(starts, counts, slab0, x, *([w] * n_w))


def _row_tile(m_rows):
    return _pick(m_rows, (768, 512, 256))


def _proj_plain(x, w, sel, col0, n_cols, name, lead_cols=0, lead_scale=1.0):
    m_rows = x.shape[0]
    tn = _pick(n_cols, (1024, 512, 256))
    assert col0 % tn == 0 and lead_cols % tn == 0
    tm = _row_tile(m_rows)

    def epi(accs, m):
        if lead_cols == 0:
            return (accs[0],)
        return (accs[0] * jnp.where(pl.program_id(0) < lead_cols // tn, lead_scale, 1.0),)

    return _mm(x, w, sel=sel, col_offs=(col0 // tn,), n_tiles=n_cols // tn, tn=tn, tm=tm,
               epi=epi,
               out_shape=[jax.ShapeDtypeStruct((m_rows, n_cols), BF16)],
               out_specs=[pl.BlockSpec((tm, tn), lambda n, m: (m, n))], name=name)[0]


def _proj_conv_in(x, w, sel):
    m_rows = x.shape[0]
    d3 = w.shape[2] // 3
    tn = _pick(d3, (512, 256))
    tm = _row_tile(m_rows)
    nt = d3 // tn
    o = jax.ShapeDtypeStruct((m_rows, d3), BF16)
    spec = pl.BlockSpec((tm, tn), lambda n, m: (m, n))
    return _mm(x, w, sel=sel, col_offs=(0, nt, 2 * nt), n_tiles=nt, tn=tn, tm=tm,
               epi=lambda accs, m: (accs[0], accs[1] * accs[2]),
               out_shape=[o, o], out_specs=[spec, spec], name="conv_in")


def _proj_gu(x, w, sel):
    m_rows = x.shape[0]
    f = w.shape[2] // 2
    tn = _pick(f, (512, 256))
    tm = _row_tile(m_rows)
    nt = f // tn
    return _mm(x, w, sel=sel, col_offs=(0, nt), n_tiles=nt, tn=tn, tm=tm,
               epi=lambda accs, m: (_silu(accs[0]) * accs[1],),
               out_shape=[jax.ShapeDtypeStruct((m_rows, f), BF16)],
               out_specs=[pl.BlockSpec((tm, tn), lambda n, m: (m, n))], name="ffn_gu")[0]


def _moe_gu(x, w, starts, counts, slab0):
    f = w.shape[2] // 2
    tn = _pick(f, (512, 256))
    nt = f // tn
    return _mm_grouped(x, w, starts, counts, slab0, col_offs=(0, nt), n_tiles=nt, tn=tn,
                       epi=lambda accs: _silu(accs[0]) * accs[1], out_dtype=BF16, n_out_cols=f, name="moe_gu")


def _moe_down(x, w, starts, counts, slab0):
    n_cols = w.shape[2]
    tn = 512
    return _mm_grouped(x, w, starts, counts, slab0, col_offs=(0,), n_tiles=n_cols // tn, tn=tn,
                       epi=lambda accs: accs[0], out_dtype=F32, n_out_cols=n_cols, name="moe_down")


def _proj_qk_rope(x, w, sel, col0, n_cols, gain, cos_t, sin_t, scale, name):
    m_rows = x.shape[0]
    tn = _pick(n_cols, (512, 256, 128))
    assert col0 % tn == 0
    tm = _row_tile(m_rows)

    def epi(accs, m, gain_ref, cos_ref, sin_ref):
        src = lax.broadcasted_iota(I32, (HEAD_DIM, HEAD_DIM), 0)
        dst = lax.broadcasted_iota(I32, (HEAD_DIM, HEAD_DIM), 1)
        perm = ((src ^ (HEAD_DIM // 4)) == dst).astype(BF16)
        cos_v, sin_v, g = cos_ref[...], sin_ref[...], gain_ref[...]
        heads = []
        for hh in range(tn // HEAD_DIM):
            y = _rms(accs[0][:, hh * HEAD_DIM:(hh + 1) * HEAD_DIM], g)
            y_hi = y.astype(BF16)
            y_lo = (y - y_hi.astype(F32)).astype(BF16)
            rot = _dot(y_hi, perm) + _dot(y_lo, perm)
            heads.append((y * cos_v + rot * sin_v) * scale)
        return (heads,)

    return _mm(x, w, sel=sel, col_offs=(col0 // tn,), n_tiles=n_cols // tn, tn=tn, tm=tm, epi=epi,
               extra=(gain.reshape(1, HEAD_DIM), cos_t, sin_t),
               extra_specs=(pl.BlockSpec((1, HEAD_DIM), lambda n, m: (0, 0)),
                            pl.BlockSpec((tm, HEAD_DIM), lambda n, m: (m, 0)),
                            pl.BlockSpec((tm, HEAD_DIM), lambda n, m: (m, 0))),
               out_shape=[jax.ShapeDtypeStruct((m_rows, n_cols), BF16)],
               out_specs=[pl.BlockSpec((tm, tn), lambda n, m: (m, n))], name=name)[0]


def _proj_resid(x, w, sel, h, gate, n_ctx, name):
    m_rows, k = x.shape
    n_cols = w.shape[2]
    big_k = k > 4096
    tn = 512 if big_k else _pick(n_cols, (1024, 512, 256))
    tm = _pick(m_rows, (512, 384, 256)) if big_k else _row_tile(m_rows)

    def epi(accs, m, h_ref, gate_ref):
        row = m * tm + lax.broadcasted_iota(I32, (tm, 1), 0)
        g = jnp.where(row < n_ctx, gate_ref[0:1, :], gate_ref[1:2, :])
        return (h_ref[...] + g * accs[0],)

    return _mm(x, w, sel=sel, col_offs=(0,), n_tiles=n_cols // tn, tn=tn, tm=tm, epi=epi,
               extra=(h, gate),
               extra_specs=(pl.BlockSpec((tm, tn), lambda n, m: (m, n)),
                            pl.BlockSpec((2, tn), lambda n, m: (0, n))),
               out_shape=[jax.ShapeDtypeStruct((m_rows, n_cols), F32)],
               out_specs=[pl.BlockSpec((tm, tn), lambda n, m: (m, n))], name=name)[0]


def _conv_gate_kernel(b_ref, u_ref, up_ref, un_ref, cw_ref, z_ref, *, tm, n_ctx_blocks, n_blocks):
    i = pl.program_id(0)
    u = u_ref[...].astype(F32)
    prev_ok = jnp.logical_and(i != 0, i != n_ctx_blocks).astype(F32)
    next_ok = jnp.logical_and(i != n_ctx_blocks - 1, i != n_blocks - 1).astype(F32)
    prow = up_ref[15:16, :].astype(F32) * prev_ok
    nrow = un_ref[0:1, :].astype(F32) * next_ok
    row = lax.broadcasted_iota(I32, (tm, 1), 0)
    um1 = jnp.where(row == 0, prow, pltpu.roll(u, 1, 0))
    up1 = jnp.where(row == tm - 1, nrow, pltpu.roll(u, tm - 1, 0))
    conv = cw_ref[0:1, :] * um1 + cw_ref[1:2, :] * u + cw_ref[2:3, :] * up1
    z_ref[...] = (b_ref[...].astype(F32) * conv).astype(z_ref.dtype)


def _conv_gate(b, u, conv_w, n_ctx):
    m, d = u.shape
    tm = ROW_BLOCK
    tc = _pick(d, (1024, 512, 256, 128))
    hb = 16
    nb = m // tm
    last_hb = m // hb - 1
    kern = functools.partial(_conv_gate_kernel, tm=tm, n_ctx_blocks=n_ctx // tm, n_blocks=nb)
    return pl.pallas_call(
        kern,
        grid=(nb, d // tc),
        in_specs=[pl.BlockSpec((tm, tc), lambda i, j: (i, j)),
                  pl.BlockSpec((tm, tc), lambda i, j: (i, j)),
                  pl.BlockSpec((hb, tc), lambda i, j: (jnp.maximum(i * (tm // hb) - 1, 0), j)),
                  pl.BlockSpec((hb, tc), lambda i, j: (jnp.minimum((i + 1) * (tm // hb), last_hb), j)),
                  pl.BlockSpec((3, tc), lambda i, j: (0, j))],
        out_specs=pl.BlockSpec((tm, tc), lambda i, j: (i, j)),
        out_shape=jax.ShapeDtypeStruct((m, d), BF16),
        compiler_params=_cparams("arbitrary", "arbitrary"),
        name="conv_gate",
    )(b, u, u, u, conv_w)


def _na_bias_tables(rpb, rows):
    nb = rows // NA_QROWS
    n_h = rpb.shape[0]
    c = np.arange(GRID_W)[:, None]
    kc = np.arange(GRID_W)[None, :]
    c0 = np.clip(c - NA_KW // 2, 0, GRID_W - NA_KW)
    c_valid = (kc >= c0) & (kc < c0 + NA_KW)
    c_sel = (kc - c + NA_KW - 1)[:, :, None] == np.arange(2 * NA_KW - 1)[None, None, :]
    c_sel = (c_sel & c_valid[:, :, None]).astype(np.float32)
    r_sel, r_valid = [], []
    for b in (0, 1, nb - 1):
        w0 = NA_QROWS * int(np.clip(b - 1, 0, nb - 3))
        ar = NA_QROWS * b + np.arange(NA_QROWS)[:, None]
        kr = w0 + np.arange(NA_WROWS)[None, :]
        r0 = np.clip(ar - NA_KH // 2, 0, rows - NA_KH)
        ok = (kr >= r0) & (kr < r0 + NA_KH)
        sel = (kr - ar + NA_KH - 1)[:, :, None] == np.arange(2 * NA_KH - 1)[None, None, :]
        r_sel.append((sel & ok[:, :, None]).astype(np.float32))
        r_valid.append(ok)
    r_sel, r_valid = np.stack(r_sel), np.stack(r_valid)
    t = jnp.einsum("pijr,hrd,ckd->phicjk", jnp.asarray(r_sel), rpb * LOG2_E, jnp.asarray(c_sel),
                   precision=lax.Precision.HIGHEST)
    valid = r_valid[:, None, :, None, :, None] & c_valid[None, None, None, :, None, :]
    t = jnp.where(valid, t, MASKED).reshape(3, n_h, NA_QROWS * GRID_W, NA_WROWS * GRID_W)
    return t


def _na_kernel(q_ref, k0_ref, k1_ref, k2_ref, kc_ref, v0_ref, v1_ref, v2_ref, vc_ref, bias_ref, o_ref):
    blk = ROW_BLOCK
    ones = jnp.ones((blk, HEAD_DIM), BF16)

    def attend(window_refs, value_refs):
        for h in range(NA_HEADS):
            hs = slice(h * HEAD_DIM, (h + 1) * HEAD_DIM)
            q = q_ref[:, hs]
            ss = [_dot_nt(q, kr[:, hs]) + bias_ref[h, :, j * blk:(j + 1) * blk] for j, kr in enumerate(window_refs)]
            ss.append(_dot_nt(q, kc_ref[:, hs]))
            mx = jnp.max(functools.reduce(jnp.maximum, ss), axis=-1, keepdims=True)
            acc = functools.reduce(jnp.add, [
                _dot(jnp.exp2(s - mx).astype(BF16), jnp.concatenate([vr[:, hs], ones], axis=1))
                for s, vr in zip(ss, value_refs)])
            o_ref[:, hs] = (acc[:, :HEAD_DIM] / acc[:, HEAD_DIM:]).astype(o_ref.dtype)

    is_ctx = pl.program_id(0) == 0

    @pl.when(is_ctx)
    def _():
        attend((), (vc_ref,))

    @pl.when(jnp.logical_not(is_ctx))
    def _():
        attend((k0_ref, k1_ref, k2_ref), (v0_ref, v1_ref, v2_ref, vc_ref))


def _na_attention(qkv, bias, n_ctx):
    m, d3 = qkv.shape
    d = d3 // 3
    blk = ROW_BLOCK
    assert n_ctx == blk and NA_QROWS * GRID_W == blk
    nq = m // blk
    nb = nq - 1

    def kv_map(j, col):
        return lambda g: (1 + jnp.clip(g - 2, 0, nb - 3) + j, col)

    def bias_map(g):
        return (jnp.where(g <= 1, 0, jnp.where(g == nq - 1, 2, 1)), 0, 0, 0)

    blkspec = lambda imap: pl.BlockSpec((blk, d), imap)
    return pl.pallas_call(
        _na_kernel,
        grid=(nq,),
        in_specs=[blkspec(lambda g: (g, 0)),
                  blkspec(kv_map(0, 1)), blkspec(kv_map(1, 1)), blkspec(kv_map(2, 1)), blkspec(lambda g: (0, 1)),
                  blkspec(kv_map(0, 2)), blkspec(kv_map(1, 2)), blkspec(kv_map(2, 2)), blkspec(lambda g: (0, 2)),
                  pl.BlockSpec((None, NA_HEADS, blk, 3 * blk), bias_map, pipeline_mode=pl.Buffered(1))],
        out_specs=blkspec(lambda g: (g, 0)),
        out_shape=jax.ShapeDtypeStruct((m, d), BF16),
        compiler_params=_cparams("arbitrary"),
        name="na_attention",
    )(qkv, qkv, qkv, qkv, qkv, qkv, qkv, qkv, qkv, bias)


def _rope_tables(n_ctx, n_lat):
    t = jnp.arange(n_lat)
    row = (t // GRID_W).astype(F32)
    col = (t % GRID_W).astype(F32)
    quarter = HEAD_DIM // 4
    inv_freq = ROPE_THETA ** (-jnp.arange(quarter, dtype=F32) / quarter)
    ar, ac = row[:, None] * inv_freq, col[:, None] * inv_freq
    cos_t = jnp.concatenate([jnp.cos(ar), jnp.cos(ar), jnp.cos(ac), jnp.cos(ac)], axis=1)
    sin_t = jnp.concatenate([-jnp.sin(ar), jnp.sin(ar), -jnp.sin(ac), jnp.sin(ac)], axis=1)
    cos_t = jnp.concatenate([jnp.ones((n_ctx, HEAD_DIM), F32), cos_t], axis=0)
    sin_t = jnp.concatenate([jnp.zeros((n_ctx, HEAD_DIM), F32), sin_t], axis=0)
    return cos_t, sin_t


def _gqa_kernel(q_ref, k_ref, v_ref, o_ref, *, n_ctx, group):
    def attend(n_keys):
        qs = [q_ref[:, g * HEAD_DIM:(g + 1) * HEAD_DIM] for g in range(group)]
        mx, acc = [None] * group, [None] * group
        for c in range(n_keys // ROW_BLOCK):
            kk = k_ref[c * ROW_BLOCK:(c + 1) * ROW_BLOCK, :]
            vv = v_ref[c * ROW_BLOCK:(c + 1) * ROW_BLOCK, :]
            for g in range(group):
                s = _dot_nt(qs[g], kk)
                row_max = jnp.max(s, axis=-1, keepdims=True)
                if c == 0:
                    mx[g] = row_max
                    acc[g] = _dot(jnp.exp2(s - row_max).astype(BF16), vv)
                else:
                    mx_new = jnp.maximum(mx[g], row_max)
                    acc[g] = jnp.exp2(mx[g] - mx_new) * acc[g] + _dot(jnp.exp2(s - mx_new).astype(BF16), vv)
                    mx[g] = mx_new
        for g in range(group):
            o_ref[:, g * HEAD_DIM:(g + 1) * HEAD_DIM] = (
                acc[g][:, :HEAD_DIM] / acc[g][:, HEAD_DIM:]).astype(o_ref.dtype)

    is_ctx = pl.program_id(1) * ROW_BLOCK < n_ctx

    @pl.when(is_ctx)
    def _():
        attend(n_ctx)

    @pl.when(jnp.logical_not(is_ctx))
    def _():
        attend(k_ref.shape[0])


def _gqa_attention(q, k, v, n_ctx):
    m, dq = q.shape
    group = GQA_HEADS // GQA_KV_HEADS
    gw = group * HEAD_DIM
    tq = ROW_BLOCK
    kern = functools.partial(_gqa_kernel, n_ctx=n_ctx, group=group)
    v = jnp.concatenate([v.reshape(m, GQA_KV_HEADS, HEAD_DIM), jnp.ones((m, GQA_KV_HEADS, HEAD_DIM), v.dtype)],
                        axis=2).reshape(m, 2 * GQA_KV_HEADS * HEAD_DIM)
    return pl.pallas_call(
        kern,
        grid=(GQA_KV_HEADS, m // tq),
        in_specs=[pl.BlockSpec((tq, gw), lambda kh, i: (i, kh)),
                  pl.BlockSpec((m, HEAD_DIM), lambda kh, i: (0, kh)),
                  pl.BlockSpec((m, 2 * HEAD_DIM), lambda kh, i: (0, kh))],
        out_specs=pl.BlockSpec((tq, gw), lambda kh, i: (i, kh)),
        out_shape=jax.ShapeDtypeStruct((m, dq), BF16),
        compiler_params=_cparams("arbitrary", "arbitrary"),
        name="gqa_attention",
    )(q, k, v)


MOE_TILE = 256


def _moe_plan(ids):
    m = ids.shape[0]
    tm = MOE_TILE
    n_rows = -(-(TOP_K * m + N_EXPERTS * (tm - 1)) // tm) * tm
    flat_e = ids.reshape(-1)
    onehot = (flat_e[:, None] == jnp.arange(N_EXPERTS, dtype=I32)[None, :]).astype(I32)
    csum = jnp.cumsum(onehot, axis=0)
    rank = jnp.sum(csum * onehot, axis=1) - 1
    tiles = (csum[-1] + tm - 1) // tm
    starts = (jnp.cumsum(tiles) - tiles) * tm
    pos = jnp.sum(starts[None, :] * onehot, axis=1) + rank
    tok = jnp.arange(TOP_K * m, dtype=I32) // TOP_K
    src = jnp.zeros((n_rows,), I32).at[pos].set(tok)
    return src, pos.astype(I32), starts.astype(I32), tiles.astype(I32)


def _prefetch_rows(idx_ref, src_hbm, buf, sem, i, n_steps, n_rows):
    def row_copy(slot, r, row):
        return pltpu.make_async_copy(src_hbm.at[pl.ds(row, 1), :], buf.at[slot, pl.ds(r, 1), :], sem.at[slot])

    def start_all(step):
        slot = step % 2
        for r in range(n_rows):
            row_copy(slot, r, idx_ref[step * n_rows + r]).start(priority=r % 2)

    @pl.when(i == 0)
    def _():
        start_all(i)

    @pl.when(i + 1 < n_steps)
    def _():
        start_all(i + 1)

    for r in range(n_rows):
        row_copy(i % 2, r, 0).wait()


def _gather_norm_kernel(src_ref, tok_ref, h_hbm, g_ref, ssc_ref, ssl_ref, o_ref, buf, sem, *, tg, n_ctx):
    i = pl.program_id(0)
    _prefetch_rows(src_ref, h_hbm, buf, sem, i, pl.num_programs(0), tg)
    y = _rms(buf[i % 2], g_ref[...])
    scale, shift = ssl_ref[1:2, :], ssl_ref[0:1, :]
    if n_ctx > 0:
        is_ctx = tok_ref[...] < n_ctx
        scale = jnp.where(is_ctx, ssc_ref[1:2, :], scale)
        shift = jnp.where(is_ctx, ssc_ref[0:1, :], shift)
    o_ref[...] = (y * (1.0 + scale) + shift).astype(o_ref.dtype)


def _gather_norm(h, src, gain, ss, n_ctx):
    m, d = h.shape
    n_rows = src.shape[0]
    tg = ROW_BLOCK
    kern = functools.partial(_gather_norm_kernel, tg=tg, n_ctx=n_ctx)
    gs = pltpu.PrefetchScalarGridSpec(
        num_scalar_prefetch=1,
        grid=(n_rows // tg,),
        in_specs=[pl.BlockSpec((tg, 1), lambda i, *pf: (i, 0)),
                  pl.BlockSpec(memory_space=pl.ANY),
                  pl.BlockSpec((1, d), lambda i, *pf: (0, 0)),
                  pl.BlockSpec((None, 2, d), lambda i, *pf: (0, 0, 0)),
                  pl.BlockSpec((None, 2, d), lambda i, *pf: (1, 0, 0))],
        out_specs=pl.BlockSpec((tg, d), lambda i, *pf: (i, 0)),
        scratch_shapes=[pltpu.VMEM((2, tg, d), F32), pltpu.SemaphoreType.DMA((2,))])
    return pl.pallas_call(kern, grid_spec=gs, out_shape=jax.ShapeDtypeStruct((n_rows, d), BF16),
                          compiler_params=_cparams("arbitrary"),
                          name="moe_gather")(src, src.reshape(n_rows, 1), h, gain.reshape(1, d), ss, ss)


def _combine_kernel(pos_ref, h_ref, gate_ref, wts_ref, y_hbm, *rest, tg, n_ctx, post):
    n_out = _N_COMBINE_OUTS[post]
    buf, sem = rest[-2:]
    outs = rest[-2 - n_out:-2]
    post_refs = rest[:-2 - n_out]
    i = pl.program_id(0)
    _prefetch_rows(pos_ref, y_hbm, buf, sem, i, pl.num_programs(0), tg * TOP_K)
    row = i * tg + lax.broadcasted_iota(I32, (tg, 1), 0)
    g = jnp.where(row < n_ctx, gate_ref[0:1, :], gate_ref[1:2, :])
    slot = i % 2
    mix = wts_ref[:, 0:1] * buf[slot, 0:tg, :] + wts_ref[:, 1:2] * buf[slot, tg:TOP_K * tg, :]
    h_new = h_ref[...] + g * mix
    if post == "final":
        outs[0][...] = _rms(h_new, post_refs[0][...])
        return
    outs[0][...] = h_new
    if post == "mod":
        ss_ref = post_refs[1]
        outs[1][...] = (_rms(h_new, post_refs[0][...]) * (1.0 + ss_ref[1:2, :]) + ss_ref[0:1, :]).astype(BF16)


_N_COMBINE_OUTS = {None: 1, "final": 1, "mod": 2}


def _combine(h, y, pos, wts, gate, n_ctx, post=None, post_gain=None, post_ss=None):
    m, d = h.shape
    tg = ROW_BLOCK
    pos_tiles = pos.reshape(m // tg, tg, TOP_K).transpose(0, 2, 1).reshape(-1)
    kern = functools.partial(_combine_kernel, tg=tg, n_ctx=n_ctx, post=post)
    row_spec = pl.BlockSpec((tg, d), lambda i, *pf: (i, 0))
    in_specs = [row_spec,
                pl.BlockSpec((2, d), lambda i, *pf: (0, 0)),
                pl.BlockSpec((tg, LANES), lambda i, *pf: (i, 0)),
                pl.BlockSpec(memory_space=pl.ANY)]
    args = [pos_tiles, h, gate, wts, y]
    out_shape = [jax.ShapeDtypeStruct((m, d), F32)]
    if post is not None:
        in_specs.append(pl.BlockSpec((1, d), lambda i, *pf: (0, 0)))
        args.append(post_gain.reshape(1, d))
    if post == "mod":
        seg = _seg_map(n_ctx // tg)
        in_specs.append(pl.BlockSpec((None, 2, d), lambda i, *pf: seg(i)))
        args.append(post_ss)
        out_shape.append(jax.ShapeDtypeStruct((m, d), BF16))
    gs = pltpu.PrefetchScalarGridSpec(
        num_scalar_prefetch=1,
        grid=(m // tg,),
        in_specs=in_specs,
        out_specs=[row_spec] * len(out_shape),
        scratch_shapes=[pltpu.VMEM((2, TOP_K * tg, d), F32), pltpu.SemaphoreType.DMA((2,))])
    res = pl.pallas_call(kern, grid_spec=gs, out_shape=out_shape,
                         compiler_params=_cparams("arbitrary"), name="moe_combine")(*args)
    return res if post == "mod" else res[0]


def _moe(h, gain, ss, gate, w_router, w_gu, w_down, j, n_ctx, **post):
    ids, wts = _router(h, gain, ss, w_router[j], n_ctx)
    src, pos, starts, tiles = _moe_plan(ids[:, :TOP_K])
    xs = _gather_norm(h, src, gain, ss, n_ctx)
    slab0 = jnp.full((1,), j * N_EXPERTS, I32)
    act = _moe_gu(xs, w_gu.reshape((-1,) + w_gu.shape[2:]), starts, tiles, slab0)
    y = _moe_down(act, w_down.reshape((-1,) + w_down.shape[2:]), starts, tiles, slab0)
    return _combine(h, y, pos, wts, gate, n_ctx, **post)


def kernel(x, c, ctx, c_ctx, ada_w, ada_b, norm_mix, norm_ffn, norm_final, conv_w_in, conv_w, conv_w_out,
           na_w_qkv, na_rpb, na_w_out, gqa_w_q, gqa_w_kv, gqa_q_norm, gqa_k_norm, gqa_w_out, ffn_w_gu,
           ffn_w_down, moe_w_router, moe_w_gu, moe_w_down):
    bsz, n_lat, d = x.shape
    n_ctx_full = ctx.shape[1]
    depth = ada_w.shape[0]
    assert bsz == 1 and n_ctx_full == ROW_BLOCK and n_lat % (GRID_W * NA_QROWS) == 0
    assert n_lat // GRID_W >= NA_WROWS

    cvec_t = jnp.zeros((d, 8), F32).at[:, 0].set(c_ctx).at[:, 1].set(c[0])
    mod = _ada(cvec_t, ada_w, ada_b)[:, :2].reshape(depth, 2, 6, d)

    h = jnp.concatenate([ctx[0], x[0]], axis=0)
    n_ctx = n_ctx_full
    cos_t, sin_t = _rope_tables(n_ctx_full, n_lat)

    a_next = None
    for i in range(depth):
        if i == depth - 1:
            h = h[n_ctx:]
            n_ctx = 0
        kind, j = i % N_MIXERS, i // N_MIXERS
        a = a_next if a_next is not None else _norm_mod(h, norm_mix[i], mod[i, :, 0:2], n_ctx)
        a_next = None
        gate1 = mod[i, :, 2]
        if kind == 0:
            b_gate, u = _proj_conv_in(a, conv_w_in, j)
            z = _conv_gate(b_gate, u, conv_w[j], n_ctx)
            h = _proj_resid(z, conv_w_out, j, h, gate1, n_ctx, "conv_out")
        elif kind == 1:
            qkv = _proj_plain(a, na_w_qkv, j, 0, 3 * d, "na_qkv", lead_cols=d, lead_scale=HEAD_DIM ** -0.5 * LOG2_E)
            bias = _na_bias_tables(na_rpb[j], n_lat // GRID_W)
            o = _na_attention(qkv, bias, n_ctx)
            h = _proj_resid(o, na_w_out, j, h, gate1, n_ctx, "na_out")
        else:
            dq = GQA_HEADS * HEAD_DIM
            dkv = GQA_KV_HEADS * HEAD_DIM
            q = _proj_qk_rope(a, gqa_w_q, j, 0, dq, gqa_q_norm[j], cos_t, sin_t, HEAD_DIM ** -0.5 * LOG2_E, "gqa_q")
            k = _proj_qk_rope(a, gqa_w_kv, j, 0, dkv, gqa_k_norm[j], cos_t, sin_t, 1.0, "gqa_k")
            v = _proj_plain(a, gqa_w_kv, j, dkv, dkv, "gqa_v")
            o = _gqa_attention(q, k, v, n_ctx)
            h = _proj_resid(o, gqa_w_out, j, h, gate1, n_ctx, "gqa_out")
        gate2 = mod[i, :, 5]
        if i % 2 == 0:
            f = _norm_mod(h, norm_ffn[i], mod[i, :, 3:5], n_ctx)
            act = _proj_gu(f, ffn_w_gu, i // 2)
            h = _proj_resid(act, ffn_w_down, i // 2, h, gate2, n_ctx, "ffn_down")
        else:
            moe_args = (h, norm_ffn[i], mod[i, :, 3:5], gate2, moe_w_router, moe_w_gu, moe_w_down, i // 2, n_ctx)
            if i == depth - 1:
                return _moe(*moe_args, post="final", post_gain=norm_final)[None]
            if i + 1 < depth - 1:
                h, a_next = _moe(*moe_args, post="mod", post_gain=norm_mix[i + 1], post_ss=mod[i + 1, :, 0:2])
            else:
                h = _moe(*moe_args)
    return _final_norm(h, norm_final)[None]
```

```python
import functools

import numpy as np
import jax
import jax.numpy as jnp
from jax import lax
from jax.experimental import pallas as pl
from jax.experimental.pallas import tpu as pltpu

F32 = jnp.float32
BF16 = jnp.bfloat16
I32 = jnp.int32

EPS = 1e-6
GRID_W = 64
N_MIXERS = 3
NA_HEADS = 16
NA_KH = 8
NA_KW = 16
NA_QROWS = 4
NA_WROWS = NA_QROWS + NA_KH
GQA_HEADS = 16
GQA_KV_HEADS = 4
HEAD_DIM = 128
ROPE_THETA = 10000.0
N_EXPERTS = 8
TOP_K = 2
LANES = 128
ROW_BLOCK = 256
MASKED = -1e30
LOG2_E = float(np.log2(np.e))
VMEM_LIMIT = 60 * 1024 * 1024


def _cparams(*sem):
    return pltpu.CompilerParams(dimension_semantics=sem, vmem_limit_bytes=VMEM_LIMIT)


def _pick(n, cands):
    for c in cands:
        if n % c == 0:
            return c
    raise ValueError(f"no tile for {n} in {cands}")


def _dot(a, b):
    return jnp.dot(a, b, preferred_element_type=F32)


def _dot_nt(a, b):
    return lax.dot_general(a, b, (((1,), (1,)), ((), ())), preferred_element_type=F32)


def _silu(x):
    return x * (1.0 / (1.0 + jnp.exp(-x)))


def _rms(x, gain):
    return x * lax.rsqrt(jnp.mean(x * x, axis=-1, keepdims=True) + EPS) * gain


def _ada_kernel(ct_ref, w_ref, b_ref, o_ref, s_scr):
    @pl.when(jnp.logical_and(pl.program_id(0) == 0, pl.program_id(1) == 0))
    def _():
        st = _silu(ct_ref[...])
        for r in range(2):
            s_scr[r] = jnp.broadcast_to(st[:, r:r + 1], s_scr.shape[1:])

    tn = o_ref.shape[1]
    row = lax.broadcasted_iota(I32, (8, LANES), 0)
    for cb in range(tn // LANES):
        w = w_ref[:, cb * LANES:(cb + 1) * LANES]
        sums = [jnp.sum(w * s_scr[r], axis=0, keepdims=True) for r in range(2)]
        tile = jnp.where(row == 0, sums[0], jnp.where(row == 1, sums[1], 0.0))
        o_ref[:, cb * LANES:(cb + 1) * LANES] = tile + b_ref[:, cb * LANES:(cb + 1) * LANES]


def _ada(cvec_t, ada_w, ada_b):
    depth, d, n = ada_w.shape
    tn = _pick(n, (1024, 512, 256, 128))
    return pl.pallas_call(
        _ada_kernel,
        grid=(depth, n // tn),
        in_specs=[pl.BlockSpec((d, 8), lambda i, j: (0, 0)),
                  pl.BlockSpec((None, d, tn), lambda i, j: (i, 0, j)),
                  pl.BlockSpec((None, 1, tn), lambda i, j: (i, 0, j))],
        out_specs=pl.BlockSpec((None, 8, tn), lambda i, j: (i, 0, j)),
        out_shape=jax.ShapeDtypeStruct((depth, 8, n), F32),
        scratch_shapes=[pltpu.VMEM((2, d, LANES), F32)],
        compiler_params=_cparams("arbitrary", "arbitrary"),
        name="ada",
    )(cvec_t, ada_w, ada_b.reshape(depth, 1, n))


def _with_modulated(h_ref, g_ref, ss_ref, n_ctx, emit):
    x = h_ref[...]
    xr = x * lax.rsqrt(jnp.mean(x * x, axis=-1, keepdims=True) + EPS)
    g = g_ref[...]
    gs_lat, sh_lat = g * (1.0 + ss_ref[1, 1:2, :]), ss_ref[1, 0:1, :]
    if n_ctx == 0:
        emit(xr * gs_lat + sh_lat)
        return
    tm = h_ref.shape[0]
    row0 = pl.program_id(0) * tm

    @pl.when(row0 < n_ctx)
    def _():
        is_ctx = row0 + lax.broadcasted_iota(I32, (tm, 1), 0) < n_ctx
        gs = jnp.where(is_ctx, g * (1.0 + ss_ref[0, 1:2, :]), gs_lat)
        emit(xr * gs + jnp.where(is_ctx, ss_ref[0, 0:1, :], sh_lat))

    @pl.when(row0 >= n_ctx)
    def _():
        emit(xr * gs_lat + sh_lat)


def _norm_mod_kernel(h_ref, g_ref, ss_ref, a_ref, *, n_ctx):
    def emit(f):
        a_ref[...] = f.astype(a_ref.dtype)

    _with_modulated(h_ref, g_ref, ss_ref, n_ctx, emit)


def _seg_map(n_ctx_blocks):
    return lambda i: (jnp.where(i < n_ctx_blocks, 0, 1), 0, 0)


def _norm_mod(h, gain, ss, n_ctx):
    m, d = h.shape
    tm = _row_tile(m)
    return pl.pallas_call(
        functools.partial(_norm_mod_kernel, n_ctx=n_ctx),
        grid=(m // tm,),
        in_specs=[pl.BlockSpec((tm, d), lambda i: (i, 0)),
                  pl.BlockSpec((1, d), lambda i: (0, 0)),
                  pl.BlockSpec((2, 2, d), lambda i: (0, 0, 0))],
        out_specs=pl.BlockSpec((tm, d), lambda i: (i, 0)),
        out_shape=jax.ShapeDtypeStruct((m, d), BF16),
        compiler_params=_cparams("arbitrary"),
        name="norm_mod",
    )(h, gain.reshape(1, d), ss)


def _router_kernel(h_ref, g_ref, ss_ref, wr_ref, ids_ref, wts_ref, *, n_ctx):
    def emit(f):
        logits = jnp.dot(f, wr_ref[...], precision=lax.Precision.HIGHEST, preferred_element_type=F32)
        lane = lax.broadcasted_iota(I32, logits.shape, 1)
        neg = jnp.float32(-jnp.inf)
        l1 = jnp.where(lane < N_EXPERTS, logits, neg)
        m1 = jnp.max(l1, axis=-1, keepdims=True)
        i1 = jnp.min(jnp.where(l1 == m1, lane, LANES), axis=-1, keepdims=True)
        l2 = jnp.where(lane == i1, neg, l1)
        m2 = jnp.max(l2, axis=-1, keepdims=True)
        i2 = jnp.min(jnp.where(l2 == m2, lane, LANES), axis=-1, keepdims=True)
        e2 = jnp.exp(m2 - m1)
        w1 = 1.0 / (1.0 + e2)
        w2 = e2 / (1.0 + e2)
        ids_ref[...] = jnp.where(lane == 0, i1, jnp.where(lane == 1, i2, 0))
        wts_ref[...] = jnp.where(lane == 0, w1, jnp.where(lane == 1, w2, 0.0))

    _with_modulated(h_ref, g_ref, ss_ref, n_ctx, emit)


def _router(h, gain, ss, w_router, n_ctx):
    m, d = h.shape
    tm = _row_tile(m)
    wr = jnp.zeros((d, LANES), F32).at[:, :N_EXPERTS].set(w_router)
    return pl.pallas_call(
        functools.partial(_router_kernel, n_ctx=n_ctx),
        grid=(m // tm,),
        in_specs=[pl.BlockSpec((tm, d), lambda i: (i, 0)),
                  pl.BlockSpec((1, d), lambda i: (0, 0)),
                  pl.BlockSpec((2, 2, d), lambda i: (0, 0, 0)),
                  pl.BlockSpec((d, LANES), lambda i: (0, 0))],
        out_specs=[pl.BlockSpec((tm, LANES), lambda i: (i, 0)),
                   pl.BlockSpec((tm, LANES), lambda i: (i, 0))],
        out_shape=[jax.ShapeDtypeStruct((m, LANES), I32),
                   jax.ShapeDtypeStruct((m, LANES), F32)],
        compiler_params=_cparams("arbitrary"),
        name="router",
    )(h, gain.reshape(1, d), ss, wr)


def _final_norm_kernel(h_ref, g_ref, o_ref):
    o_ref[...] = _rms(h_ref[...], g_ref[...])


def _final_norm(h, gain):
    m, d = h.shape
    tm = ROW_BLOCK
    return pl.pallas_call(
        _final_norm_kernel,
        grid=(m // tm,),
        in_specs=[pl.BlockSpec((tm, d), lambda i: (i, 0)),
                  pl.BlockSpec((1, d), lambda i: (0, 0))],
        out_specs=pl.BlockSpec((tm, d), lambda i: (i, 0)),
        out_shape=jax.ShapeDtypeStruct((m, d), F32),
        compiler_params=_cparams("arbitrary"),
        name="final_norm",
    )(h, gain.reshape(1, d))


def _mm(x, w, *, sel, col_offs, n_tiles, tn, tm, epi, extra=(), extra_specs=(), out_shape, out_specs, name):
    m_rows, k = x.shape
    n_w = len(col_offs)
    n_ex = len(extra)
    n_out = len(out_shape)

    def kern(*refs):
        x_ref = refs[0]
        w_refs = refs[1:1 + n_w]
        ex = refs[1 + n_w:1 + n_w + n_ex]
        outs = refs[1 + n_w + n_ex:1 + n_w + n_ex + n_out]
        wb = refs[-1]
        m = pl.program_id(1)

        @pl.when(m == 0)
        def _():
            for j in range(n_w):
                wb[j] = w_refs[j][...].astype(BF16)

        xv = x_ref[...]
        accs = [_dot(xv, wb[j]) for j in range(n_w)]
        for o, r in zip(outs, epi(accs, m, *ex)):
            if isinstance(r, list):
                c0 = 0
                for piece in r:
                    o[:, c0:c0 + piece.shape[1]] = piece.astype(o.dtype)
                    c0 += piece.shape[1]
            else:
                o[...] = r.astype(o.dtype)

    def w_map(off):
        return lambda n, m: (sel, 0, n + off)

    in_specs = ([pl.BlockSpec((tm, k), lambda n, m: (m, 0))]
                + [pl.BlockSpec((None, k, tn), w_map(off)) for off in col_offs]
                + list(extra_specs))
    return pl.pallas_call(kern, grid=(n_tiles, m_rows // tm), in_specs=in_specs, out_specs=out_specs,
                          out_shape=out_shape, scratch_shapes=[pltpu.VMEM((n_w, k, tn), BF16)],
                          compiler_params=_cparams("arbitrary", "arbitrary"), name=name)(x, *([w] * n_w), *extra)


X_AHEAD = 3


def _mm_grouped(x, w, starts, counts, slab0, *, col_offs, n_tiles, tn, epi, out_dtype, n_out_cols, name):
    n_rows, k = x.shape
    n_w = len(col_offs)
    ts = MOE_TILE

    def kern(st_ref, ct_ref, s0_ref, x_hbm, *rest):
        w_refs = rest[:n_w]
        o_hbm = rest[n_w]
        xbuf, obuf, wb, pend, sin, sout = rest[n_w + 1:]
        n, e = pl.program_id(0), pl.program_id(1)
        base, cnt = st_ref[e], ct_ref[e]

        def x_copy_from(first_row, r, slot):
            row0 = pl.multiple_of(first_row + r * ts, ts)
            return pltpu.make_async_copy(x_hbm.at[pl.ds(row0, ts), :], xbuf.at[slot], sin.at[slot])

        def x_copy(r, slot):
            return x_copy_from(base, r, slot)

        def x_start_head(first_row, n_tiles_here, cond):
            for r0 in range(X_AHEAD):
                @pl.when(jnp.logical_and(cond, r0 < n_tiles_here))
                def _():
                    x_copy_from(first_row, r0, r0).start()

        def o_copy(r, slot):
            row0 = pl.multiple_of(base + r * ts, ts)
            col0 = pl.multiple_of(n * tn, tn)
            return pltpu.make_async_copy(obuf.at[slot], o_hbm.at[pl.ds(row0, ts), pl.ds(col0, tn)], sout.at[slot])

        def o_drain(slot):
            @pl.when(pend[slot] == 1)
            def _():
                o_copy(0, slot).wait()
                pend[slot] = 0

        def o_emit(r, slot, tile):
            o_drain(slot)
            obuf[slot] = tile
            o_copy(r, slot).start()
            pend[slot] = 1

        first_step = jnp.logical_and(n == 0, e == 0)
        last_step = jnp.logical_and(n == pl.num_programs(0) - 1, e == pl.num_programs(1) - 1)

        @pl.when(first_step)
        def _():
            pend[0] = 0
            pend[1] = 0

        x_start_head(base, cnt, first_step)

        for j in range(n_w):
            wb[j] = w_refs[j][...].astype(BF16)

        def body(r, carry):
            slot = r % (X_AHEAD + 1)

            @pl.when(r + X_AHEAD < cnt)
            def _():
                x_copy(r + X_AHEAD, (r + X_AHEAD) % (X_AHEAD + 1)).start()

            x_copy(r, slot).wait()
            xv = xbuf[slot]
            o_emit(r, r % 2, epi([_dot(xv, wb[j]) for j in range(n_w)]).astype(out_dtype))
            return carry

        lax.fori_loop(0, cnt, body, 0)

        e_next = jnp.where(e == pl.num_programs(1) - 1, 0, e + 1)
        x_start_head(st_ref[e_next], ct_ref[e_next], jnp.logical_not(last_step))

        @pl.when(e == pl.num_programs(1) - 1)
        def _():
            def zero_tile(r, carry):
                o_emit(r, r % 2, jnp.zeros((ts, tn), out_dtype))
                return carry

            lax.fori_loop(cnt, (n_rows - base) // ts, zero_tile, 0)

        @pl.when(last_step)
        def _():
            o_drain(0)
            o_drain(1)

    def w_map(off):
        return lambda n, e, st, ct, s0: (s0[0] + e, 0, n + off)

    gs = pltpu.PrefetchScalarGridSpec(
        num_scalar_prefetch=3,
        grid=(n_tiles, N_EXPERTS),
        in_specs=[pl.BlockSpec(memory_space=pl.ANY)] + [pl.BlockSpec((None, k, tn), w_map(off)) for off in col_offs],
        out_specs=pl.BlockSpec(memory_space=pl.ANY),
        scratch_shapes=[pltpu.VMEM((X_AHEAD + 1, ts, k), BF16), pltpu.VMEM((2, ts, tn), out_dtype),
                        pltpu.VMEM((n_w, k, tn), BF16), pltpu.SMEM((2,), I32),
                        pltpu.SemaphoreType.DMA((X_AHEAD + 1,)), pltpu.SemaphoreType.DMA((2,))])
    return pl.pallas_call(kern, grid_spec=gs, out_shape=jax.ShapeDtypeStruct((n_rows, n_out_cols), out_dtype),
                          compiler_params=_cparams("arbitrary", "arbitrary"),
                          name=name)(starts, counts, slab0, x, *([w] * n_w))


def _row_tile(m_rows):
    return _pick(m_rows, (768, 512, 256))


def _proj_plain(x, w, sel, col0, n_cols, name, lead_cols=0, lead_scale=1.0):
    m_rows = x.shape[0]
    tn = _pick(n_cols, (1024, 512, 256))
    assert col0 % tn == 0 and lead_cols % tn == 0
    tm = _row_tile(m_rows)

    def epi(accs, m):
        if lead_cols == 0:
            return (accs[0],)
        return (accs[0] * jnp.where(pl.program_id(0) < lead_cols // tn, lead_scale, 1.0),)

    return _mm(x, w, sel=sel, col_offs=(col0 // tn,), n_tiles=n_cols // tn, tn=tn, tm=tm,
               epi=epi,
               out_shape=[jax.ShapeDtypeStruct((m_rows, n_cols), BF16)],
               out_specs=[pl.BlockSpec((tm, tn), lambda n, m: (m, n))], name=name)[0]


def _proj_conv_in(x, w, sel):
    m_rows = x.shape[0]
    d3 = w.shape[2] // 3
    tn = _pick(d3, (512, 256))
    tm = _row_tile(m_rows)
    nt = d3 // tn
    o = jax.ShapeDtypeStruct((m_rows, d3), BF16)
    spec = pl.BlockSpec((tm, tn), lambda n, m: (m, n))
    return _mm(x, w, sel=sel, col_offs=(0, nt, 2 * nt), n_tiles=nt, tn=tn, tm=tm,
               epi=lambda accs, m: (accs[0], accs[1] * accs[2]),
               out_shape=[o, o], out_specs=[spec, spec], name="conv_in")


def _proj_gu(x, w, sel):
    m_rows = x.shape[0]
    f = w.shape[2] // 2
    tn = _pick(f, (512, 256))
    tm = _row_tile(m_rows)
    nt = f // tn
    return _mm(x, w, sel=sel, col_offs=(0, nt), n_tiles=nt, tn=tn, tm=tm,
               epi=lambda accs, m: (_silu(accs[0]) * accs[1],),
               out_shape=[jax.ShapeDtypeStruct((m_rows, f), BF16)],
               out_specs=[pl.BlockSpec((tm, tn), lambda n, m: (m, n))], name="ffn_gu")[0]


def _moe_gu(x, w, starts, counts, slab0):
    f = w.shape[2] // 2
    tn = _pick(f, (512, 256))
    nt = f // tn
    return _mm_grouped(x, w, starts, counts, slab0, col_offs=(0, nt), n_tiles=nt, tn=tn,
                       epi=lambda accs: _silu(accs[0]) * accs[1], out_dtype=BF16, n_out_cols=f, name="moe_gu")


def _moe_down(x, w, starts, counts, slab0):
    n_cols = w.shape[2]
    tn = 512
    return _mm_grouped(x, w, starts, counts, slab0, col_offs=(0,), n_tiles=n_cols // tn, tn=tn,
                       epi=lambda accs: accs[0], out_dtype=F32, n_out_cols=n_cols, name="moe_down")


def _proj_qk_rope(x, w, sel, col0, n_cols, gain, cos_t, sin_t, scale, name):
    m_rows = x.shape[0]
    tn = _pick(n_cols, (512, 256, 128))
    assert col0 % tn == 0
    tm = _row_tile(m_rows)

    def epi(accs, m, gain_ref, cos_ref, sin_ref):
        src = lax.broadcasted_iota(I32, (HEAD_DIM, HEAD_DIM), 0)
        dst = lax.broadcasted_iota(I32, (HEAD_DIM, HEAD_DIM), 1)
        perm = ((src ^ (HEAD_DIM // 4)) == dst).astype(BF16)
        cos_v, sin_v, g = cos_ref[...], sin_ref[...], gain_ref[...]
        heads = []
        for hh in range(tn // HEAD_DIM):
            y = _rms(accs[0][:, hh * HEAD_DIM:(hh + 1) * HEAD_DIM], g)
            y_hi = y.astype(BF16)
            y_lo = (y - y_hi.astype(F32)).astype(BF16)
            rot = _dot(y_hi, perm) + _dot(y_lo, perm)
            heads.append((y * cos_v + rot * sin_v) * scale)
        return (heads,)

    return _mm(x, w, sel=sel, col_offs=(col0 // tn,), n_tiles=n_cols // tn, tn=tn, tm=tm, epi=epi,
               extra=(gain.reshape(1, HEAD_DIM), cos_t, sin_t),
               extra_specs=(pl.BlockSpec((1, HEAD_DIM), lambda n, m: (0, 0)),
                            pl.BlockSpec((tm, HEAD_DIM), lambda n, m: (m, 0)),
                            pl.BlockSpec((tm, HEAD_DIM), lambda n, m: (m, 0))),
               out_shape=[jax.ShapeDtypeStruct((m_rows, n_cols), BF16)],
               out_specs=[pl.BlockSpec((tm, tn), lambda n, m: (m, n))], name=name)[0]


def _proj_resid(x, w, sel, h, gate, n_ctx, name):
    m_rows, k = x.shape
    n_cols = w.shape[2]
    big_k = k > 4096
    tn = 512 if big_k else _pick(n_cols, (1024, 512, 256))
    tm = _pick(m_rows, (512, 384, 256)) if big_k else _row_tile(m_rows)

    def epi(accs, m, h_ref, gate_ref):
        row = m * tm + lax.broadcasted_iota(I32, (tm, 1), 0)
        g = jnp.where(row < n_ctx, gate_ref[0:1, :], gate_ref[1:2, :])
        return (h_ref[...] + g * accs[0],)

    return _mm(x, w, sel=sel, col_offs=(0,), n_tiles=n_cols // tn, tn=tn, tm=tm, epi=epi,
               extra=(h, gate),
               extra_specs=(pl.BlockSpec((tm, tn), lambda n, m: (m, n)),
                            pl.BlockSpec((2, tn), lambda n, m: (0, n))),
               out_shape=[jax.ShapeDtypeStruct((m_rows, n_cols), F32)],
               out_specs=[pl.BlockSpec((tm, tn), lambda n, m: (m, n))], name=name)[0]


def _conv_gate_kernel(b_ref, u_ref, up_ref, un_ref, cw_ref, z_ref, *, tm, n_ctx_blocks, n_blocks):
    i = pl.program_id(0)
    u = u_ref[...].astype(F32)
    prev_ok = jnp.logical_and(i != 0, i != n_ctx_blocks).astype(F32)
    next_ok = jnp.logical_and(i != n_ctx_blocks - 1, i != n_blocks - 1).astype(F32)
    prow = up_ref[15:16, :].astype(F32) * prev_ok
    nrow = un_ref[0:1, :].astype(F32) * next_ok
    row = lax.broadcasted_iota(I32, (tm, 1), 0)
    um1 = jnp.where(row == 0, prow, pltpu.roll(u, 1, 0))
    up1 = jnp.where(row == tm - 1, nrow, pltpu.roll(u, tm - 1, 0))
    conv = cw_ref[0:1, :] * um1 + cw_ref[1:2, :] * u + cw_ref[2:3, :] * up1
    z_ref[...] = (b_ref[...].astype(F32) * conv).astype(z_ref.dtype)


def _conv_gate(b, u, conv_w, n_ctx):
    m, d = u.shape
    tm = ROW_BLOCK
    tc = _pick(d, (1024, 512, 256, 128))
    hb = 16
    nb = m // tm
    last_hb = m // hb - 1
    kern = functools.partial(_conv_gate_kernel, tm=tm, n_ctx_blocks=n_ctx // tm, n_blocks=nb)
    return pl.pallas_call(
        kern,
        grid=(nb, d // tc),
        in_specs=[pl.BlockSpec((tm, tc), lambda i, j: (i, j)),
                  pl.BlockSpec((tm, tc), lambda i, j: (i, j)),
                  pl.BlockSpec((hb, tc), lambda i, j: (jnp.maximum(i * (tm // hb) - 1, 0), j)),
                  pl.BlockSpec((hb, tc), lambda i, j: (jnp.minimum((i + 1) * (tm // hb), last_hb), j)),
                  pl.BlockSpec((3, tc), lambda i, j: (0, j))],
        out_specs=pl.BlockSpec((tm, tc), lambda i, j: (i, j)),
        out_shape=jax.ShapeDtypeStruct((m, d), BF16),
        compiler_params=_cparams("arbitrary", "arbitrary"),
        name="conv_gate",
    )(b, u, u, u, conv_w)


def _na_bias_tables(rpb, rows):
    nb = rows // NA_QROWS
    n_h = rpb.shape[0]
    c = np.arange(GRID_W)[:, None]
    kc = np.arange(GRID_W)[None, :]
    c0 = np.clip(c - NA_KW // 2, 0, GRID_W - NA_KW)
    c_valid = (kc >= c0) & (kc < c0 + NA_KW)
    c_sel = (kc - c + NA_KW - 1)[:, :, None] == np.arange(2 * NA_KW - 1)[None, None, :]
    c_sel = (c_sel & c_valid[:, :, None]).astype(np.float32)
    r_sel, r_valid = [], []
    for b in (0, 1, nb - 1):
        w0 = NA_QROWS * int(np.clip(b - 1, 0, nb - 3))
        ar = NA_QROWS * b + np.arange(NA_QROWS)[:, None]
        kr = w0 + np.arange(NA_WROWS)[None, :]
        r0 = np.clip(ar - NA_KH // 2, 0, rows - NA_KH)
        ok = (kr >= r0) & (kr < r0 + NA_KH)
        sel = (kr - ar + NA_KH - 1)[:, :, None] == np.arange(2 * NA_KH - 1)[None, None, :]
        r_sel.append((sel & ok[:, :, None]).astype(np.float32))
        r_valid.append(ok)
    r_sel, r_valid = np.stack(r_sel), np.stack(r_valid)
    t = jnp.einsum("pijr,hrd,ckd->phicjk", jnp.asarray(r_sel), rpb * LOG2_E, jnp.asarray(c_sel),
                   precision=lax.Precision.HIGHEST)
    valid = r_valid[:, None, :, None, :, None] & c_valid[None, None, None, :, None, :]
    t = jnp.where(valid, t, MASKED).reshape(3, n_h, NA_QROWS * GRID_W, NA_WROWS * GRID_W)
    return t


def _na_kernel(q_ref, k0_ref, k1_ref, k2_ref, kc_ref, v0_ref, v1_ref, v2_ref, vc_ref, bias_ref, o_ref):
    blk = ROW_BLOCK
    ones = jnp.ones((blk, HEAD_DIM), BF16)

    def attend(window_refs, value_refs):
        for h in range(NA_HEADS):
            hs = slice(h * HEAD_DIM, (h + 1) * HEAD_DIM)
            q = q_ref[:, hs]
            ss = [_dot_nt(q, kr[:, hs]) + bias_ref[h, :, j * blk:(j + 1) * blk] for j, kr in enumerate(window_refs)]
            ss.append(_dot_nt(q, kc_ref[:, hs]))
            mx = jnp.max(functools.reduce(jnp.maximum, ss), axis=-1, keepdims=True)
            acc = functools.reduce(jnp.add, [
                _dot(jnp.exp2(s - mx).astype(BF16), jnp.concatenate([vr[:, hs], ones], axis=1))
                for s, vr in zip(ss, value_refs)])
            o_ref[:, hs] = (acc[:, :HEAD_DIM] / acc[:, HEAD_DIM:]).astype(o_ref.dtype)

    is_ctx = pl.program_id(0) == 0

    @pl.when(is_ctx)
    def _():
        attend((), (vc_ref,))

    @pl.when(jnp.logical_not(is_ctx))
    def _():
        attend((k0_ref, k1_ref, k2_ref), (v0_ref, v1_ref, v2_ref, vc_ref))


def _na_attention(qkv, bias, n_ctx):
    m, d3 = qkv.shape
    d = d3 // 3
    blk = ROW_BLOCK
    assert n_ctx == blk and NA_QROWS * GRID_W == blk
    nq = m // blk
    nb = nq - 1

    def kv_map(j, col):
        return lambda g: (1 + jnp.clip(g - 2, 0, nb - 3) + j, col)

    def bias_map(g):
        return (jnp.where(g <= 1, 0, jnp.where(g == nq - 1, 2, 1)), 0, 0, 0)

    blkspec = lambda imap: pl.BlockSpec((blk, d), imap)
    return pl.pallas_call(
        _na_kernel,
        grid=(nq,),
        in_specs=[blkspec(lambda g: (g, 0)),
                  blkspec(kv_map(0, 1)), blkspec(kv_map(1, 1)), blkspec(kv_map(2, 1)), blkspec(lambda g: (0, 1)),
                  blkspec(kv_map(0, 2)), blkspec(kv_map(1, 2)), blkspec(kv_map(2, 2)), blkspec(lambda g: (0, 2)),
                  pl.BlockSpec((None, NA_HEADS, blk, 3 * blk), bias_map, pipeline_mode=pl.Buffered(1))],
        out_specs=blkspec(lambda g: (g, 0)),
        out_shape=jax.ShapeDtypeStruct((m, d), BF16),
        compiler_params=_cparams("arbitrary"),
        name="na_attention",
    )(qkv, qkv, qkv, qkv, qkv, qkv, qkv, qkv, qkv, bias)


def _rope_tables(n_ctx, n_lat):
    t = jnp.arange(n_lat)
    row = (t // GRID_W).astype(F32)
    col = (t % GRID_W).astype(F32)
    quarter = HEAD_DIM // 4
    inv_freq = ROPE_THETA ** (-jnp.arange(quarter, dtype=F32) / quarter)
    ar, ac = row[:, None] * inv_freq, col[:, None] * inv_freq
    cos_t = jnp.concatenate([jnp.cos(ar), jnp.cos(ar), jnp.cos(ac), jnp.cos(ac)], axis=1)
    sin_t = jnp.concatenate([-jnp.sin(ar), jnp.sin(ar), -jnp.sin(ac), jnp.sin(ac)], axis=1)
    cos_t = jnp.concatenate([jnp.ones((n_ctx, HEAD_DIM), F32), cos_t], axis=0)
    sin_t = jnp.concatenate([jnp.zeros((n_ctx, HEAD_DIM), F32), sin_t], axis=0)
    return cos_t, sin_t


def _gqa_kernel(q_ref, k_ref, v_ref, o_ref, *, n_ctx, group):
    def attend(n_keys):
        qs = [q_ref[:, g * HEAD_DIM:(g + 1) * HEAD_DIM] for g in range(group)]
        mx, acc = [None] * group, [None] * group
        for c in range(n_keys // ROW_BLOCK):
            kk = k_ref[c * ROW_BLOCK:(c + 1) * ROW_BLOCK, :]
            vv = v_ref[c * ROW_BLOCK:(c + 1) * ROW_BLOCK, :]
            for g in range(group):
                s = _dot_nt(qs[g], kk)
                row_max = jnp.max(s, axis=-1, keepdims=True)
                if c == 0:
                    mx[g] = row_max
                    acc[g] = _dot(jnp.exp2(s - row_max).astype(BF16), vv)
                else:
                    mx_new = jnp.maximum(mx[g], row_max)
                    acc[g] = jnp.exp2(mx[g] - mx_new) * acc[g] + _dot(jnp.exp2(s - mx_new).astype(BF16), vv)
                    mx[g] = mx_new
        for g in range(group):
            o_ref[:, g * HEAD_DIM:(g + 1) * HEAD_DIM] = (
                acc[g][:, :HEAD_DIM] / acc[g][:, HEAD_DIM:]).astype(o_ref.dtype)

    is_ctx = pl.program_id(1) * ROW_BLOCK < n_ctx

    @pl.when(is_ctx)
    def _():
        attend(n_ctx)

    @pl.when(jnp.logical_not(is_ctx))
    def _():
        attend(k_ref.shape[0])


def _gqa_attention(q, k, v, n_ctx):
    m, dq = q.shape
    group = GQA_HEADS // GQA_KV_HEADS
    gw = group * HEAD_DIM
    tq = ROW_BLOCK
    kern = functools.partial(_gqa_kernel, n_ctx=n_ctx, group=group)
    v = jnp.concatenate([v.reshape(m, GQA_KV_HEADS, HEAD_DIM), jnp.ones((m, GQA_KV_HEADS, HEAD_DIM), v.dtype)],
                        axis=2).reshape(m, 2 * GQA_KV_HEADS * HEAD_DIM)
    return pl.pallas_call(
        kern,
        grid=(GQA_KV_HEADS, m // tq),
        in_specs=[pl.BlockSpec((tq, gw), lambda kh, i: (i, kh)),
                  pl.BlockSpec((m, HEAD_DIM), lambda kh, i: (0, kh)),
                  pl.BlockSpec((m, 2 * HEAD_DIM), lambda kh, i: (0, kh))],
        out_specs=pl.BlockSpec((tq, gw), lambda kh, i: (i, kh)),
        out_shape=jax.ShapeDtypeStruct((m, dq), BF16),
        compiler_params=_cparams("arbitrary", "arbitrary"),
        name="gqa_attention",
    )(q, k, v)


MOE_TILE = 256


def _moe_plan(ids):
    m = ids.shape[0]
    tm = MOE_TILE
    n_rows = -(-(TOP_K * m + N_EXPERTS * (tm - 1)) // tm) * tm
    flat_e = ids.reshape(-1)
    onehot = (flat_e[:, None] == jnp.arange(N_EXPERTS, dtype=I32)[None, :]).astype(I32)
    csum = jnp.cumsum(onehot, axis=0)
    rank = jnp.sum(csum * onehot, axis=1) - 1
    tiles = (csum[-1] + tm - 1) // tm
    starts = (jnp.cumsum(tiles) - tiles) * tm
    pos = jnp.sum(starts[None, :] * onehot, axis=1) + rank
    tok = jnp.arange(TOP_K * m, dtype=I32) // TOP_K
    src = jnp.zeros((n_rows,), I32).at[pos].set(tok)
    return src, pos.astype(I32), starts.astype(I32), tiles.astype(I32)


def _prefetch_rows(idx_ref, src_hbm, buf, sem, i, n_steps, n_rows):
    def row_copy(slot, r, row):
        return pltpu.make_async_copy(src_hbm.at[pl.ds(row, 1), :], buf.at[slot, pl.ds(r, 1), :], sem.at[slot])

    def start_all(step):
        slot = step % 2
        for r in range(n_rows):
            row_copy(slot, r, idx_ref[step * n_rows + r]).start(priority=r % 2)

    @pl.when(i == 0)
    def _():
        start_all(i)

    @pl.when(i + 1 < n_steps)
    def _():
        start_all(i + 1)

    for r in range(n_rows):
        row_copy(i % 2, r, 0).wait()


def _gather_norm_kernel(src_ref, tok_ref, h_hbm, g_ref, ssc_ref, ssl_ref, o_ref, buf, sem, *, tg, n_ctx):
    i = pl.program_id(0)
    _prefetch_rows(src_ref, h_hbm, buf, sem, i, pl.num_programs(0), tg)
    x = buf[i % 2]
    xr = x * lax.rsqrt(jnp.mean(x * x, axis=-1, keepdims=True) + EPS)
    g = g_ref[...]
    gs_lat, sh_lat = g * (1.0 + ssl_ref[1:2, :]), ssl_ref[0:1, :]
    if n_ctx == 0:
        o_ref[...] = (xr * gs_lat + sh_lat).astype(o_ref.dtype)
        return
    has_ctx = jnp.minimum(src_ref[i * tg], src_ref[i * tg + tg - 1]) < n_ctx

    @pl.when(has_ctx)
    def _():
        is_ctx = tok_ref[...] < n_ctx
        gs = jnp.where(is_ctx, g * (1.0 + ssc_ref[1:2, :]), gs_lat)
        o_ref[...] = (xr * gs + jnp.where(is_ctx, ssc_ref[0:1, :], sh_lat)).astype(o_ref.dtype)

    @pl.when(jnp.logical_not(has_ctx))
    def _():
        o_ref[...] = (xr * gs_lat + sh_lat).astype(o_ref.dtype)


def _gather_norm(h, src, gain, ss, n_ctx):
    m, d = h.shape
    n_rows = src.shape[0]
    tg = ROW_BLOCK
    kern = functools.partial(_gather_norm_kernel, tg=tg, n_ctx=n_ctx)
    gs = pltpu.PrefetchScalarGridSpec(
        num_scalar_prefetch=1,
        grid=(n_rows // tg,),
        in_specs=[pl.BlockSpec((tg, 1), lambda i, *pf: (i, 0)),
                  pl.BlockSpec(memory_space=pl.ANY),
                  pl.BlockSpec((1, d), lambda i, *pf: (0, 0)),
                  pl.BlockSpec((None, 2, d), lambda i, *pf: (0, 0, 0)),
                  pl.BlockSpec((None, 2, d), lambda i, *pf: (1, 0, 0))],
        out_specs=pl.BlockSpec((tg, d), lambda i, *pf: (i, 0)),
        scratch_shapes=[pltpu.VMEM((2, tg, d), F32), pltpu.SemaphoreType.DMA((2,))])
    return pl.pallas_call(kern, grid_spec=gs, out_shape=jax.ShapeDtypeStruct((n_rows, d), BF16),
                          compiler_params=_cparams("arbitrary"),
                          name="moe_gather")(src, src.reshape(n_rows, 1), h, gain.reshape(1, d), ss, ss)


def _combine_kernel(pos_ref, h_ref, gate_ref, wts_ref, y_hbm, *rest, tg, n_ctx, post):
    n_out = _N_COMBINE_OUTS[post]
    buf, sem = rest[-2:]
    outs = rest[-2 - n_out:-2]
    post_refs = rest[:-2 - n_out]
    i = pl.program_id(0)
    _prefetch_rows(pos_ref, y_hbm, buf, sem, i, pl.num_programs(0), tg * TOP_K)
    row = i * tg + lax.broadcasted_iota(I32, (tg, 1), 0)
    g = jnp.where(row < n_ctx, gate_ref[0:1, :], gate_ref[1:2, :])
    slot = i % 2
    mix = wts_ref[:, 0:1] * buf[slot, 0:tg, :] + wts_ref[:, 1:2] * buf[slot, tg:TOP_K * tg, :]
    h_new = h_ref[...] + g * mix
    if post == "final":
        outs[0][...] = _rms(h_new, post_refs[0][...])
        return
    outs[0][...] = h_new
    if post == "mod":
        ss_ref = post_refs[1]
        outs[1][...] = (_rms(h_new, post_refs[0][...]) * (1.0 + ss_ref[1:2, :]) + ss_ref[0:1, :]).astype(BF16)


_N_COMBINE_OUTS = {None: 1, "final": 1, "mod": 2}


def _combine(h, y, pos, wts, gate, n_ctx, post=None, post_gain=None, post_ss=None):
    m, d = h.shape
    tg = ROW_BLOCK
    pos_tiles = pos.reshape(m // tg, tg, TOP_K).transpose(0, 2, 1).reshape(-1)
    kern = functools.partial(_combine_kernel, tg=tg, n_ctx=n_ctx, post=post)
    row_spec = pl.BlockSpec((tg, d), lambda i, *pf: (i, 0))
    in_specs = [row_spec,
                pl.BlockSpec((2, d), lambda i, *pf: (0, 0)),
                pl.BlockSpec((tg, LANES), lambda i, *pf: (i, 0)),
                pl.BlockSpec(memory_space=pl.ANY)]
    args = [pos_tiles, h, gate, wts, y]
    out_shape = [jax.ShapeDtypeStruct((m, d), F32)]
    if post is not None:
        in_specs.append(pl.BlockSpec((1, d), lambda i, *pf: (0, 0)))
        args.append(post_gain.reshape(1, d))
    if post == "mod":
        seg = _seg_map(n_ctx // tg)
        in_specs.append(pl.BlockSpec((None, 2, d), lambda i, *pf: seg(i)))
        args.append(post_ss)
        out_shape.append(jax.ShapeDtypeStruct((m, d), BF16))
    gs = pltpu.PrefetchScalarGridSpec(
        num_scalar_prefetch=1,
        grid=(m // tg,),
        in_specs=in_specs,
        out_specs=[row_spec] * len(out_shape),
        scratch_shapes=[pltpu.VMEM((2, TOP_K * tg, d), F32), pltpu.SemaphoreType.DMA((2,))])
    res = pl.pallas_call(kern, grid_spec=gs, out_shape=out_shape,
                         compiler_params=_cparams("arbitrary"), name="moe_combine")(*args)
    return res if post == "mod" else res[0]


def _moe(h, gain, ss, gate, w_router, w_gu, w_down, j, n_ctx, **post):
    ids, wts = _router(h, gain, ss, w_router[j], n_ctx)
    src, pos, starts, tiles = _moe_plan(ids[:, :TOP_K])
    xs = _gather_norm(h, src, gain, ss, n_ctx)
    slab0 = jnp.full((1,), j * N_EXPERTS, I32)
    act = _moe_gu(xs, w_gu.reshape((-1,) + w_gu.shape[2:]), starts, tiles, slab0)
    y = _moe_down(act, w_down.reshape((-1,) + w_down.shape[2:]), starts, tiles, slab0)
    return _combine(h, y, pos, wts, gate, n_ctx, **post)


def kernel(x, c, ctx, c_ctx, ada_w, ada_b, norm_mix, norm_ffn, norm_final, conv_w_in, conv_w, conv_w_out,
           na_w_qkv, na_rpb, na_w_out, gqa_w_q, gqa_w_kv, gqa_q_norm, gqa_k_norm, gqa_w_out, ffn_w_gu,
           ffn_w_down, moe_w_router, moe_w_gu, moe_w_down):
    bsz, n_lat, d = x.shape
    n_ctx_full = ctx.shape[1]
    depth = ada_w.shape[0]
    assert bsz == 1 and n_ctx_full == ROW_BLOCK and n_lat % (GRID_W * NA_QROWS) == 0
    assert n_lat // GRID_W >= NA_WROWS

    cvec_t = jnp.zeros((d, 8), F32).at[:, 0].set(c_ctx).at[:, 1].set(c[0])
    mod = _ada(cvec_t, ada_w, ada_b)[:, :2].reshape(depth, 2, 6, d)

    h = jnp.concatenate([ctx[0], x[0]], axis=0)
    n_ctx = n_ctx_full
    cos_t, sin_t = _rope_tables(n_ctx_full, n_lat)

    a_next = None
    for i in range(depth):
        if i == depth - 1:
            h = h[n_ctx:]
            n_ctx = 0
        kind, j = i % N_MIXERS, i // N_MIXERS
        a = a_next if a_next is not None else _norm_mod(h, norm_mix[i], mod[i, :, 0:2], n_ctx)
        a_next = None
        gate1 = mod[i, :, 2]
        if kind == 0:
            b_gate, u = _proj_conv_in(a, conv_w_in, j)
            z = _conv_gate(b_gate, u, conv_w[j], n_ctx)
            h = _proj_resid(z, conv_w_out, j, h, gate1, n_ctx, "conv_out")
        elif kind == 1:
            qkv = _proj_plain(a, na_w_qkv, j, 0, 3 * d, "na_qkv", lead_cols=d, lead_scale=HEAD_DIM ** -0.5 * LOG2_E)
            bias = _na_bias_tables(na_rpb[j], n_lat // GRID_W)
            o = _na_attention(qkv, bias, n_ctx)
            h = _proj_resid(o, na_w_out, j, h, gate1, n_ctx, "na_out")
        else:
            dq = GQA_HEADS * HEAD_DIM
            dkv = GQA_KV_HEADS * HEAD_DIM
            q = _proj_qk_rope(a, gqa_w_q, j, 0, dq, gqa_q_norm[j], cos_t, sin_t, HEAD_DIM ** -0.5 * LOG2_E, "gqa_q")
            k = _proj_qk_rope(a, gqa_w_kv, j, 0, dkv, gqa_k_norm[j], cos_t, sin_t, 1.0, "gqa_k")
            v = _proj_plain(a, gqa_w_kv, j, dkv, dkv, "gqa_v")
            o = _gqa_attention(q, k, v, n_ctx)
            h = _proj_resid(o, gqa_w_out, j, h, gate1, n_ctx, "gqa_out")
        gate2 = mod[i, :, 5]
        if i % 2 == 0:
            f = _norm_mod(h, norm_ffn[i], mod[i, :, 3:5], n_ctx)
            act = _proj_gu(f, ffn_w_gu, i // 2)
            h = _proj_resid(act, ffn_w_down, i // 2, h, gate2, n_ctx, "ffn_down")
        else:
            moe_args = (h, norm_ffn[i], mod[i, :, 3:5], gate2, moe_w_router, moe_w_gu, moe_w_down, i // 2, n_ctx)
            if i == depth - 1:
                return _moe(*moe_args, post="final", post_gain=norm_final)[None]
            if i + 1 < depth - 1:
                h, a_next = _moe(*moe_args, post="mod", post_gain=norm_mix[i + 1], post_ss=mod[i + 1, :, 0:2])
            else:
                h = _moe(*moe_args)
    return _final_norm(h, norm_final)[None]
```
